```python
import math
import jax, jax.numpy as jnp
from jax import lax
import numpy as np

D_MODEL = 4096
BATCH = 2
SEQ = 8192
DEPTH = 1

NSA_HEADS = 16
NSA_KV_GROUPS = 4
HEAD_DIM = 128
CMP_BLOCK = 32
CMP_STRIDE = 16
CMP_HIDDEN = 256
SLC_BLOCK = 64
SLC_TOPK = 16
WINDOW = 512
Q_BLOCK = 128
NSA_WIDTH = NSA_HEADS * HEAD_DIM
KV_WIDTH = NSA_KV_GROUPS * HEAD_DIM
SGU_GROUPS = 16
SGU_GROUP_DIM = 128
SGU_CHUNK = 128
SGU_WIDTH = SGU_GROUPS * SGU_GROUP_DIM
REL_BUCKETS = 32
REL_MAX_DIST = 128
D_FF = 11008
CONV_WIDTH = 3
EPS = 1e-6
NEG = -1e30
FORCE = 1e6
IN_SPLITS = (NSA_WIDTH, KV_WIDTH, KV_WIDTH, KV_WIDTH, KV_WIDTH, KV_WIDTH, KV_WIDTH,
             NSA_HEADS * 3, 2 * SGU_WIDTH, D_MODEL, D_MODEL)
IN_WIDTH = sum(IN_SPLITS)

kernel_name = "hybrid_nsa_sgu_convffn_block"


def rms_norm(x, g):
    xf = x.astype(jnp.float32)
    y = xf * lax.rsqrt(jnp.mean(xf * xf, axis=-1, keepdims=True) + EPS)
    return (y * g.astype(jnp.float32)).astype(x.dtype)


def layer_norm(x, g, b):
    xf = x.astype(jnp.float32)
    mu = jnp.mean(xf, axis=-1, keepdims=True)
    var = jnp.mean(jnp.square(xf - mu), axis=-1, keepdims=True)
    y = (xf - mu) * lax.rsqrt(var + EPS) * g.astype(jnp.float32) + b.astype(jnp.float32)
    return y.astype(x.dtype)


def modulate(h, shift, scale):
    return h * (1 + scale[:, None, :]) + shift[:, None, :]


def rel_bucket(dist):
    n = jnp.maximum(dist, 0)
    max_exact = REL_BUCKETS // 2
    nf = jnp.maximum(n, 1).astype(jnp.float32)
    large = max_exact + (jnp.log(nf / max_exact) / math.log(REL_MAX_DIST / max_exact)
                         * (REL_BUCKETS - max_exact)).astype(jnp.int32)
    large = jnp.minimum(large, REL_BUCKETS - 1)
    return jnp.where(n < max_exact, n, large)


def selection_matrix(n_cmp, n_slc):
    ratio = SLC_BLOCK // CMP_STRIDE
    span = CMP_BLOCK // CMP_STRIDE
    d = jnp.arange(n_cmp)[:, None] - ratio * jnp.arange(n_slc)[None, :]
    b = jnp.arange(span)
    a = d[..., None] + b
    return jnp.sum((a >= 0) & (a < ratio), axis=-1).astype(jnp.float32)


def compress_blocks(kv, pos, w1, b1, w2, b2):
    B, T, G, dk = kv.shape
    n_cmp = (T - CMP_BLOCK) // CMP_STRIDE + 1
    idx = jnp.arange(n_cmp)[:, None] * CMP_STRIDE + jnp.arange(CMP_BLOCK)[None, :]
    blocks = kv[:, idx] + pos[:, None, :]
    blocks = blocks.transpose(0, 3, 1, 2, 4).reshape(B, G, n_cmp, CMP_BLOCK * dk)
    hid = jax.nn.gelu(blocks @ w1 + b1)
    return hid @ w2 + b2


def nsa_attention(q, kc, vc, ks, vs, kw, vw, gates, rel_table):
    B, T, H, dk = q.shape
    G = kc.shape[1]
    R = H // G
    n_cmp = kc.shape[2]
    n_slc = T // SLC_BLOCK
    top_k = min(SLC_TOPK, n_slc)
    n_sel = top_k * SLC_BLOCK
    qg = q.reshape(B, T, G, R, dk).transpose(0, 2, 3, 1, 4) * (HEAD_DIM ** -0.5)
    gg = gates.reshape(B, T, G, R, 3).transpose(0, 2, 3, 1, 4)
    ks_b = ks.transpose(0, 2, 1, 3).reshape(B, G, n_slc, SLC_BLOCK, dk)
    vs_b = vs.transpose(0, 2, 1, 3).reshape(B, G, n_slc, SLC_BLOCK, dk)
    pad = ((0, 0), (0, 0), (WINDOW, 0), (0, 0))
    kw_p = jnp.pad(kw.transpose(0, 2, 1, 3), pad)
    vw_p = jnp.pad(vw.transpose(0, 2, 1, 3), pad)
    sel_map = selection_matrix(n_cmp, n_slc)
    cmp_end = jnp.arange(n_cmp) * CMP_STRIDE + CMP_BLOCK - 1
    slc_start = jnp.arange(n_slc) * SLC_BLOCK
    tbl_grp = rel_table.reshape(REL_BUCKETS, G, R).transpose(1, 0, 2)
    b_idx = jnp.arange(B)[:, None, None, None]
    g_idx = jnp.arange(G)[None, :, None, None]
    q_idx = jnp.arange(Q_BLOCK)[None, None, :, None]

    def heads_bias(dist):
        bias = rel_table[rel_bucket(dist)]
        return bias.transpose(2, 0, 1).reshape(G, R, dist.shape[0], dist.shape[1])

    def block(qb):
        t0 = qb * Q_BLOCK
        tq = t0 + jnp.arange(Q_BLOCK)
        qq = lax.dynamic_slice_in_dim(qg, t0, Q_BLOCK, axis=3)
        gq = lax.dynamic_slice_in_dim(gg, t0, Q_BLOCK, axis=3)
        dist_c = tq[:, None] - cmp_end[None, :]
        vis_c = dist_c >= 0
        s_c = jnp.einsum('bgrqd,bgnd->bgrqn', qq, kc).astype(jnp.float32) + heads_bias(dist_c)
        s_c = jnp.where(vis_c, s_c, NEG)
        p_c = jnp.where(jnp.any(vis_c, axis=-1)[:, None], jax.nn.softmax(s_c, axis=-1), 0.0)
        o_c = jnp.einsum('bgrqn,bgnd->bgrqd', p_c.astype(vc.dtype), vc)
        imp = jnp.einsum('bgrqn,nj->bgqj', p_c, sel_map)
        cur = tq // SLC_BLOCK
        vis_s = slc_start[None, :] <= tq[:, None]
        j = jnp.arange(n_slc)[None, :]
        forced = (j == 0) | (j == cur[:, None]) | (j == cur[:, None] - 1)
        score = jnp.where(vis_s & forced, FORCE, jnp.where(vis_s, imp, -1.0))
        _, sel = lax.top_k(score, top_k)
        sel_ok = vis_s[q_idx, sel]
        kg = ks_b[b_idx, g_idx, sel]
        vg = vs_b[b_idx, g_idx, sel]
        tok = sel[..., None] * SLC_BLOCK + jnp.arange(SLC_BLOCK)
        dist_s = tq[:, None, None] - tok
        ok_s = sel_ok[..., None] & (dist_s >= 0)
        bias_s = tbl_grp[g_idx[..., None], rel_bucket(dist_s)].transpose(0, 1, 5, 2, 3, 4)
        s_s = jnp.einsum('bgrqd,bgqkld->bgrqkl', qq, kg).astype(jnp.float32) + bias_s
        s_s = jnp.where(ok_s[:, :, None], s_s, NEG).reshape(B, G, R, Q_BLOCK, n_sel)
        p_s = jax.nn.softmax(s_s, axis=-1)
        o_s = jnp.einsum('bgrqm,bgqmd->bgrqd', p_s.astype(vg.dtype), vg.reshape(B, G, Q_BLOCK, n_sel, dk))
        kwb = lax.dynamic_slice_in_dim(kw_p, t0, Q_BLOCK + WINDOW, axis=2)
        vwb = lax.dynamic_slice_in_dim(vw_p, t0, Q_BLOCK + WINDOW, axis=2)
        kpos = t0 - WINDOW + jnp.arange(Q_BLOCK + WINDOW)
        dist_w = tq[:, None] - kpos[None, :]
        ok_w = (dist_w >= 0) & (dist_w < WINDOW) & (kpos[None, :] >= 0)
        s_w = jnp.einsum('bgrqd,bgkd->bgrqk', qq, kwb).astype(jnp.float32) + heads_bias(dist_w)
        p_w = jax.nn.softmax(jnp.where(ok_w, s_w, NEG), axis=-1)
        o_w = jnp.einsum('bgrqk,bgkd->bgrqd', p_w.astype(vwb.dtype), vwb)
        o = gq[..., 0:1] * o_c + gq[..., 1:2] * o_s + gq[..., 2:3] * o_w
        return o.transpose(0, 3, 1, 2, 4).reshape(B, Q_BLOCK, H * dk)

    out = lax.map(block, jnp.arange(T // Q_BLOCK))
    return out.transpose(1, 0, 2, 3).reshape(B, T, H * dk)


def spatial_gating(z, ln_g, ln_b, w_s, b_s):
    B, T, _ = z.shape
    u, v = jnp.split(jax.nn.gelu(z), 2, axis=-1)
    v = layer_norm(v, ln_g, ln_b)
    v = v.reshape(B, T // SGU_CHUNK, SGU_CHUNK, SGU_GROUPS, SGU_GROUP_DIM)
    w = w_s * jnp.tril(jnp.ones((SGU_CHUNK, SGU_CHUNK), w_s.dtype))
    mixed = jnp.einsum('hts,bcshd->bcthd', w, v) + b_s.T[:, :, None]
    return u * mixed.reshape(B, T, SGU_WIDTH)


def token_mixer(h, w_in, cmp_pos, cmp_w1, cmp_b1, cmp_w2, cmp_b2, rel_table,
                sgu_ln_g, sgu_ln_b, sgu_w, sgu_b, w_proj_nsa, w_proj_sgu, w_out):
    B, T, _ = h.shape
    proj = h @ w_in
    (q, k_c, v_c, k_s, v_s, k_w, v_w, g_nsa, z_sgu, g_a, g_b) = jnp.split(
        proj, list(np.cumsum(IN_SPLITS)[:-1]), axis=-1)
    kvshape = (B, T, NSA_KV_GROUPS, HEAD_DIM)
    kc = compress_blocks(k_c.reshape(kvshape), cmp_pos[0], cmp_w1[0], cmp_b1[0], cmp_w2[0], cmp_b2[0])
    vc = compress_blocks(v_c.reshape(kvshape), cmp_pos[1], cmp_w1[1], cmp_b1[1], cmp_w2[1], cmp_b2[1])
    gates = jax.nn.sigmoid(g_nsa).reshape(B, T, NSA_HEADS, 3)
    y_a = nsa_attention(q.reshape(B, T, NSA_HEADS, HEAD_DIM), kc, vc,
                        k_s.reshape(kvshape), v_s.reshape(kvshape),
                        k_w.reshape(kvshape), v_w.reshape(kvshape), gates, rel_table)
    y_b = spatial_gating(z_sgu, sgu_ln_g, sgu_ln_b, sgu_w, sgu_b)
    merged = jax.nn.sigmoid(g_a) * (y_a @ w_proj_nsa) + jax.nn.sigmoid(g_b) * (y_b @ w_proj_sgu)
    return merged @ w_out


def conv_ffn(h, w_up, conv_w, conv_b, w_down):
    a = h @ w_up
    T = a.shape[1]
    ap = jnp.pad(a, ((0, 0), (CONV_WIDTH - 1, 0), (0, 0)))
    conv = conv_b
    for k in range(CONV_WIDTH):
        conv = conv + ap[:, k:k + T] * conv_w[k]
    gate, up = jnp.split(conv, 2, axis=-1)
    return (jax.nn.silu(gate) * up) @ w_down


def setup_inputs(seed: int = 0) -> dict:
    key = jax.random.key(seed)
    ks = jax.random.split(key, 24)
    L = DEPTH
    f32 = jnp.float32

    def nrm(k, shape, scale):
        return jax.random.normal(k, shape, f32) * scale

    return {
        "x": nrm(ks[0], (BATCH, SEQ, D_MODEL), 1.0),
        "c": nrm(ks[1], (BATCH, D_MODEL), 1.0),
        "w_mod": nrm(ks[2], (L, D_MODEL, 6 * D_MODEL), D_MODEL ** -0.5),
        "b_mod": nrm(ks[3], (L, 6 * D_MODEL), 0.01),
        "g_norms": 1.0 + nrm(ks[4], (L, 4, D_MODEL), 0.01),
        "w_in": nrm(ks[5], (L, D_MODEL, IN_WIDTH), D_MODEL ** -0.5),
        "cmp_pos": nrm(ks[6], (L, 2, CMP_BLOCK, HEAD_DIM), 0.02),
        "cmp_w1": nrm(ks[7], (L, 2, CMP_BLOCK * HEAD_DIM, CMP_HIDDEN), (CMP_BLOCK * HEAD_DIM) ** -0.5),
        "cmp_b1": nrm(ks[8], (L, 2, CMP_HIDDEN), 0.01),
        "cmp_w2": nrm(ks[9], (L, 2, CMP_HIDDEN, HEAD_DIM), CMP_HIDDEN ** -0.5),
        "cmp_b2": nrm(ks[10], (L, 2, HEAD_DIM), 0.01),
        "rel_table": nrm(ks[11], (REL_BUCKETS, NSA_HEADS), 0.5),
        "sgu_ln_g": 1.0 + nrm(ks[12], (L, SGU_WIDTH), 0.01),
        "sgu_ln_b": nrm(ks[13], (L, SGU_WIDTH), 0.01),
        "sgu_w": nrm(ks[14], (L, SGU_GROUPS, SGU_CHUNK, SGU_CHUNK), SGU_CHUNK ** -0.5),
        "sgu_b": 1.0 + nrm(ks[15], (L, SGU_GROUPS, SGU_CHUNK), 0.01),
        "w_proj_nsa": nrm(ks[16], (L, NSA_WIDTH, D_MODEL), NSA_WIDTH ** -0.5),
        "w_proj_sgu": nrm(ks[17], (L, SGU_WIDTH, D_MODEL), SGU_WIDTH ** -0.5),
        "w_out": nrm(ks[18], (L, D_MODEL, D_MODEL), D_MODEL ** -0.5),
        "w_ffn_up": nrm(ks[19], (L, D_MODEL, 2 * D_FF), D_MODEL ** -0.5),
        "ffn_conv_w": nrm(ks[20], (L, CONV_WIDTH, 2 * D_FF), CONV_WIDTH ** -0.5),
        "ffn_conv_b": nrm(ks[21], (L, 2 * D_FF), 0.01),
        "w_ffn_down": nrm(ks[22], (L, D_FF, D_MODEL), D_FF ** -0.5),
    }


def reference(x, c, w_mod, b_mod, g_norms, w_in, cmp_pos, cmp_w1, cmp_b1, cmp_w2, cmp_b2,
              rel_table, sgu_ln_g, sgu_ln_b, sgu_w, sgu_b, w_proj_nsa, w_proj_sgu, w_out,
              w_ffn_up, ffn_conv_w, ffn_conv_b, w_ffn_down):
    c_act = jax.nn.silu(c)
    for l in range(DEPTH):
        mod = c_act @ w_mod[l] + b_mod[l]
        sh1, sc1, gt1, sh2, sc2, gt2 = jnp.split(mod, 6, axis=-1)
        h = modulate(rms_norm(x, g_norms[l, 0]), sh1, sc1)
        y = token_mixer(h, w_in[l], cmp_pos[l], cmp_w1[l], cmp_b1[l], cmp_w2[l], cmp_b2[l], rel_table,
                        sgu_ln_g[l], sgu_ln_b[l], sgu_w[l], sgu_b[l], w_proj_nsa[l], w_proj_sgu[l], w_out[l])
        x = x + gt1[:, None, :] * rms_norm(y, g_norms[l, 1])
        h = modulate(rms_norm(x, g_norms[l, 2]), sh2, sc2)
        y = conv_ffn(h, w_ffn_up[l], ffn_conv_w[l], ffn_conv_b[l], w_ffn_down[l])
        x = x + gt2[:, None, :] * rms_norm(y, g_norms[l, 3])
    return x
```

```python
import functools
import math

import jax
import jax.numpy as jnp
import numpy as np
from jax.experimental import pallas as pl
from jax.experimental.pallas import tpu as pltpu

NSA_HEADS = 16
NSA_KV_GROUPS = 4
HEADS_PER_GROUP = NSA_HEADS // NSA_KV_GROUPS
HEAD_DIM = 128
CMP_BLOCK = 32
CMP_STRIDE = 16
CMP_HIDDEN = 256
SLC_BLOCK = 64
SLC_TOPK = 16
WINDOW = 512
Q_BLOCK = 128
SGU_GROUPS = 16
SGU_GROUP_DIM = 128
SGU_CHUNK = 128
SGU_WIDTH = SGU_GROUPS * SGU_GROUP_DIM
REL_BUCKETS = 32
REL_MAX_DIST = 128
CONV_WIDTH = 3
EPS = 1e-6
NEG = -1e30
FORCE = 1e6

NSA_WIDTH = NSA_HEADS * HEAD_DIM
KV_WIDTH = NSA_KV_GROUPS * HEAD_DIM
LANES = 128
QL = HEADS_PER_GROUP * Q_BLOCK
CMP_FRONT = 16
CMP_NEAR = 32
FFN_HALF_TILE = 256

MXU_DTYPE = jnp.bfloat16
F32 = jnp.float32
VMEM_LIMIT = 56 * 1024 * 1024


def _cparams(sem):
    return pltpu.CompilerParams(dimension_semantics=sem, vmem_limit_bytes=VMEM_LIMIT)


def _mod_kernel(c_ref, w_ref, b_ref, o_ref):
    c = c_ref[...]
    ca = c * jax.nn.sigmoid(c)
    o_ref[...] = jnp.dot(ca.astype(MXU_DTYPE), w_ref[...].astype(MXU_DTYPE),
                         preferred_element_type=F32) + b_ref[...]


def _mod(c8, w_mod, b_mod, tn=512):
    rows, d = c8.shape
    n = w_mod.shape[1]
    return pl.pallas_call(
        _mod_kernel,
        grid=(n // tn,),
        in_specs=[pl.BlockSpec((rows, d), lambda j: (0, 0)),
                  pl.BlockSpec((d, tn), lambda j: (0, j)),
                  pl.BlockSpec((1, tn), lambda j: (0, j))],
        out_specs=pl.BlockSpec((rows, tn), lambda j: (0, j)),
        out_shape=jax.ShapeDtypeStruct((rows, n), F32),
        compiler_params=_cparams(("arbitrary",)),
        name="mod",
    )(c8, w_mod, b_mod)


def _rms(x, g):
    return x * jax.lax.rsqrt(jnp.mean(x * x, axis=-1, keepdims=True) + EPS) * g


def _norm_mod_kernel(x_ref, g_ref, sh_ref, sc_ref, o_ref):
    y = _rms(x_ref[0], g_ref[...])
    o_ref[0] = (y * (1.0 + sc_ref[0]) + sh_ref[0]).astype(o_ref.dtype)


def _norm_mod(x, g, sh, sc, tm=256):
    b, t, d = x.shape
    row = pl.BlockSpec((1, 1, d), lambda i, j: (i, 0, 0))
    return pl.pallas_call(
        _norm_mod_kernel,
        grid=(b, t // tm),
        in_specs=[pl.BlockSpec((1, tm, d), lambda i, j: (i, j, 0)),
                  pl.BlockSpec((1, d), lambda i, j: (0, 0)), row, row],
        out_specs=pl.BlockSpec((1, tm, d), lambda i, j: (i, j, 0)),
        out_shape=jax.ShapeDtypeStruct((b, t, d), MXU_DTYPE),
        compiler_params=_cparams(("arbitrary", "arbitrary")),
        name="norm_mod",
    )(x, g, sh, sc)


def _resid_norm_mod_kernel(x_ref, y_ref, g1_ref, gt_ref, g2_ref, sh_ref, sc_ref, x1_ref, h_ref):
    x1 = x_ref[0] + gt_ref[0] * _rms(y_ref[0], g1_ref[...])
    x1_ref[0] = x1
    h_ref[0] = (_rms(x1, g2_ref[...]) * (1.0 + sc_ref[0]) + sh_ref[0]).astype(h_ref.dtype)


def _resid_norm_mod(x, y, g1, gt, g2, sh, sc, tm=256):
    b, t, d = x.shape
    row = pl.BlockSpec((1, 1, d), lambda i, j: (i, 0, 0))
    gspec = pl.BlockSpec((1, d), lambda i, j: (0, 0))
    blk = pl.BlockSpec((1, tm, d), lambda i, j: (i, j, 0))
    return pl.pallas_call(
        _resid_norm_mod_kernel,
        grid=(b, t // tm),
        in_specs=[blk, blk, gspec, row, gspec, row, row],
        out_specs=[blk, blk],
        out_shape=[jax.ShapeDtypeStruct((b, t, d), F32), jax.ShapeDtypeStruct((b, t, d), MXU_DTYPE)],
        compiler_params=_cparams(("arbitrary", "arbitrary")),
        name="resid_norm_mod",
    )(x, y, g1, gt, g2, sh, sc)


def _resid_norm_kernel(x_ref, y_ref, g_ref, gt_ref, o_ref):
    o_ref[0] = x_ref[0] + gt_ref[0] * _rms(y_ref[0], g_ref[...])


def _resid_norm(x, y, g, gt, tm=256):
    b, t, d = x.shape
    blk = pl.BlockSpec((1, tm, d), lambda i, j: (i, j, 0))
    return pl.pallas_call(
        _resid_norm_kernel,
        grid=(b, t // tm),
        in_specs=[blk, blk, pl.BlockSpec((1, d), lambda i, j: (0, 0)),
                  pl.BlockSpec((1, 1, d), lambda i, j: (i, 0, 0))],
        out_specs=blk,
        out_shape=jax.ShapeDtypeStruct((b, t, d), F32),
        compiler_params=_cparams(("arbitrary", "arbitrary")),
        name="resid_norm",
    )(x, y, g, gt)


def _mm_kernel(*refs, nk, scaled):
    if scaled:
        x_ref, w_ref, s_ref, o_ref = refs[:4]
        rest = refs[4:]
    else:
        x_ref, w_ref, o_ref = refs[:3]
        s_ref = None
        rest = refs[3:]

    def finish(acc):
        if scaled:
            acc = acc * s_ref[...]
        o_ref[...] = acc.astype(o_ref.dtype)

    part = jnp.dot(x_ref[...], w_ref[...], preferred_element_type=F32)
    if nk == 1:
        finish(part)
        return
    acc_ref, = rest
    k = pl.program_id(2)

    @pl.when(k == 0)
    def _():
        acc_ref[...] = part

    @pl.when(k > 0)
    def _():
        acc_ref[...] += part

    @pl.when(k == nk - 1)
    def _():
        finish(acc_ref[...])


def _matmul(x, w, *, tm, tn, tk, out_dtype, col_scale=None, name="matmul"):
    m, kdim = x.shape
    n = w.shape[1]
    nk = kdim // tk
    in_specs = [pl.BlockSpec((tm, tk), lambda i, j, k: (i, k)),
                pl.BlockSpec((tk, tn), lambda i, j, k: (k, j))]
    args = [x, w]
    if col_scale is not None:
        in_specs.append(pl.BlockSpec((1, tn), lambda i, j, k: (0, j)))
        args.append(col_scale)
    return pl.pallas_call(
        functools.partial(_mm_kernel, nk=nk, scaled=col_scale is not None),
        grid=(m // tm, n // tn, nk),
        in_specs=in_specs,
        out_specs=pl.BlockSpec((tm, tn), lambda i, j, k: (i, j)),
        out_shape=jax.ShapeDtypeStruct((m, n), out_dtype),
        scratch_shapes=[] if nk == 1 else [pltpu.VMEM((tm, tn), F32)],
        compiler_params=_cparams(("arbitrary", "arbitrary", "arbitrary")),
        name=name,
    )(*args)


def _compress_kernel(a_ref, w1_ref, pos_ref, b1_ref, w2_ref, b2_ref, o_ref, *, n_cmp):
    a = a_ref[0, 0, 0]
    w1 = w1_ref[0]
    half = a.shape[1]
    top = jnp.dot(a, w1[:half], preferred_element_type=F32)
    bot = jnp.dot(a, w1[half:], preferred_element_type=F32)
    rows = a.shape[0]
    pre = top + pltpu.roll(bot, rows - 1, 0)
    pos_term = jnp.dot(pos_ref[0].astype(MXU_DTYPE), w1, preferred_element_type=F32)[0:1]
    hid = jax.nn.gelu(pre + pos_term + b1_ref[0])
    out = jnp.dot(hid.astype(MXU_DTYPE), w2_ref[0], preferred_element_type=F32) + b2_ref[0]
    ridx = jax.lax.broadcasted_iota(jnp.int32, out.shape, 0)
    o_ref[0, 0, 0] = jnp.where(ridx < n_cmp, out, 0.0)


def _compress(a, w1, pos8, b1, w2, b2, n_cmp):
    b, two, g, rows, width = a.shape
    hid = w1.shape[2]
    dk = w2.shape[2]
    return pl.pallas_call(
        functools.partial(_compress_kernel, n_cmp=n_cmp),
        grid=(two, b, g),
        in_specs=[pl.BlockSpec((1, 1, 1, rows, width), lambda s, i, j: (i, s, j, 0, 0)),
                  pl.BlockSpec((1, 2 * width, hid), lambda s, i, j: (s, 0, 0)),
                  pl.BlockSpec((1, 8, 2 * width), lambda s, i, j: (s, 0, 0)),
                  pl.BlockSpec((1, 1, hid), lambda s, i, j: (s, 0, 0)),
                  pl.BlockSpec((1, hid, dk), lambda s, i, j: (s, 0, 0)),
                  pl.BlockSpec((1, 1, dk), lambda s, i, j: (s, 0, 0))],
        out_specs=pl.BlockSpec((1, 1, 1, rows, dk), lambda s, i, j: (i, s, j, 0, 0)),
        out_shape=jax.ShapeDtypeStruct((b, two, g, rows, dk), F32),
        compiler_params=_cparams(("arbitrary", "arbitrary", "arbitrary")),
        name="compress",
    )(a, w1, pos8, b1, w2, b2)


def _rel_bucket_np(dist):
    n = np.maximum(dist, 0)
    max_exact = REL_BUCKETS // 2
    nf = np.maximum(n, 1).astype(np.float32)
    large = max_exact + (np.log(nf / max_exact) / math.log(REL_MAX_DIST / max_exact)
                         * (REL_BUCKETS - max_exact)).astype(np.int32)
    large = np.minimum(large, REL_BUCKETS - 1)
    return np.where(n < max_exact, n, large).astype(np.int32)


def _bias_index_tiles():
    k = np.arange(Q_BLOCK)[:, None]
    i = np.arange(Q_BLOCK)[None, :]
    d_diag = i - k
    diag = np.where(d_diag >= 0, _rel_bucket_np(d_diag), -1)
    prev = _rel_bucket_np(i - k + Q_BLOCK)
    m = np.arange(CMP_NEAR)[:, None]
    d_c = i - CMP_STRIDE * m + (CMP_STRIDE * CMP_FRONT - CMP_BLOCK + 1)
    near = np.where(d_c >= 0, _rel_bucket_np(d_c), -1)
    return diag.astype(np.int32), prev.astype(np.int32), near.astype(np.int32)


def _bias_kernel(tab_ref, idd_ref, idp_ref, idc_ref, od_ref, op_ref, oc_ref, o31_ref):
    h = pl.program_id(0)

    def lut(idx):
        out = jnp.full(idx.shape, NEG, F32)
        for b in range(REL_BUCKETS):
            out = jnp.where(idx == b, tab_ref[b, h], out)
        return out

    od_ref[0] = lut(idd_ref[...])
    op_ref[0] = lut(idp_ref[...])
    oc_ref[0] = lut(idc_ref[...])
    o31_ref[0] = jnp.full(o31_ref.shape[1:], tab_ref[REL_BUCKETS - 1, h], F32)


def _bias_tiles(rel_table):
    diag, prev, near = _bias_index_tiles()
    g, r = NSA_KV_GROUPS, HEADS_PER_GROUP
    full = lambda a: pl.BlockSpec(a.shape, lambda h: (0, 0))
    head = lambda rows: pl.BlockSpec((1, rows, Q_BLOCK), lambda h: (h // r, 0, h % r))
    return pl.pallas_call(
        _bias_kernel,
        grid=(NSA_HEADS,),
        in_specs=[pl.BlockSpec(memory_space=pltpu.SMEM), full(diag), full(prev), full(near)],
        out_specs=[head(Q_BLOCK), head(Q_BLOCK), head(CMP_NEAR), head(8)],
        out_shape=[jax.ShapeDtypeStruct((g, Q_BLOCK, QL), F32), jax.ShapeDtypeStruct((g, Q_BLOCK, QL), F32),
                   jax.ShapeDtypeStruct((g, CMP_NEAR, QL), F32), jax.ShapeDtypeStruct((g, 8, QL), F32)],
        compiler_params=_cparams(("arbitrary",)),
        name="bias_tiles",
    )(rel_table, jnp.asarray(diag), jnp.asarray(prev), jnp.asarray(near))


def _selection_map_t(n_cmp, n_slc, ncp):
    ratio = SLC_BLOCK // CMP_STRIDE
    span = CMP_BLOCK // CMP_STRIDE
    d = np.arange(n_cmp)[:, None] - ratio * np.arange(n_slc)[None, :]
    a = d[..., None] + np.arange(span)
    sel = np.sum((a >= 0) & (a < ratio), axis=-1).astype(np.float32)
    out = np.zeros((n_slc, ncp), np.float32)
    out[:, CMP_FRONT:CMP_FRONT + n_cmp] = sel.T
    return out


def _nsa_kernel(q_ref, kc_ref, vct_ref, selt_ref, ks_ref, vst_ref, kw_ref, vwt_ref, g_ref,
                bdiag_ref, bprev_ref, bc_ref, c31_ref, o_ref,
                p_scr, selb_scr, m_scr, l_scr, acc_scr, *, top_k):
    qb = pl.program_id(2)
    q = q_ref[0, 0, 0]
    ncp = kc_ref.shape[2]
    n_slc = selt_ref.shape[0]
    c31 = c31_ref[0][0:1, :]
    lane_i = jax.lax.broadcasted_iota(jnp.int32, (1, QL), 1) & (Q_BLOCK - 1)
    t0 = qb * Q_BLOCK

    s_all = jnp.dot(kc_ref[0, 0].astype(MXU_DTYPE), q, preferred_element_type=F32)
    prow = jax.lax.broadcasted_iota(jnp.int32, (ncp, QL), 0)
    near0 = pl.multiple_of(qb * (Q_BLOCK // CMP_STRIDE), 8)
    s_far = jnp.where(prow < near0, jnp.where(prow >= CMP_FRONT, s_all + c31, NEG), NEG)
    kn = kc_ref[0, 0, pl.ds(near0, CMP_NEAR), :]
    mrow = jax.lax.broadcasted_iota(jnp.int32, (CMP_NEAR, QL), 0)
    s_near = jnp.dot(kn.astype(MXU_DTYPE), q, preferred_element_type=F32) + bc_ref[0]
    s_near = jnp.where(mrow >= CMP_FRONT - near0, s_near, NEG)
    mx = jnp.maximum(jnp.max(s_far, axis=0, keepdims=True), jnp.max(s_near, axis=0, keepdims=True))
    e_far = jnp.exp(s_far - mx)
    e_near = jnp.exp(s_near - mx)
    den = jnp.sum(e_far, axis=0, keepdims=True) + jnp.sum(e_near, axis=0, keepdims=True)
    inv = jnp.where(t0 + lane_i >= CMP_BLOCK - 1, 1.0 / den, 0.0)
    p_scr[...] = e_far * inv
    p_scr[pl.ds(near0, CMP_NEAR), :] = e_near * inv
    p = p_scr[...]
    o_c = jnp.dot(vct_ref[0, 0], p.astype(MXU_DTYPE), preferred_element_type=F32)

    psum = p[:, 0:Q_BLOCK]
    for r in range(1, HEADS_PER_GROUP):
        psum = psum + p[:, r * Q_BLOCK:(r + 1) * Q_BLOCK]
    p_hi = psum.astype(MXU_DTYPE)
    p_lo = (psum - p_hi.astype(F32)).astype(MXU_DTYPE)
    selt = selt_ref[...]
    imp = (jnp.dot(selt, p_hi, preferred_element_type=F32)
           + jnp.dot(selt, p_lo, preferred_element_type=F32))
    jidx = jax.lax.broadcasted_iota(jnp.int32, (n_slc, Q_BLOCK), 0).astype(F32)
    iidx = jax.lax.broadcasted_iota(jnp.int32, (n_slc, Q_BLOCK), 1)
    cur = (qb * (Q_BLOCK // SLC_BLOCK) + jax.lax.shift_right_logical(iidx, int(math.log2(SLC_BLOCK)))).astype(F32)
    vis = jidx <= cur
    forced = jnp.where(jidx == 0.0, 1.0, jnp.where(jidx == cur, 1.0, jnp.where(jidx == cur - 1.0, 1.0, 0.0)))
    score = jnp.where(vis, jnp.where(forced > 0.5, FORCE, imp), -1.0)
    rem = score
    chosen = jnp.zeros((n_slc, Q_BLOCK), F32)
    for _ in range(top_k):
        best = jnp.max(rem, axis=0, keepdims=True)
        first = jnp.min(jnp.where(rem == best, jidx, float(n_slc)), axis=0, keepdims=True)
        hit = jidx == first
        chosen = jnp.where(hit, 1.0, chosen)
        rem = jnp.where(hit, -2.0, rem)
    selb = jnp.where(vis, jnp.where(chosen > 0.5, 0.0, NEG), NEG)
    selb_scr[...] = jnp.concatenate([selb] * HEADS_PER_GROUP, axis=1)

    def reset():
        m_scr[...] = jnp.full(m_scr.shape, NEG, F32)
        l_scr[...] = jnp.zeros(l_scr.shape, F32)
        acc_scr[...] = jnp.zeros(acc_scr.shape, F32)

    def step(k_tile, vt_tile, bias):
        s = jnp.dot(k_tile, q, preferred_element_type=F32) + bias
        m_old = m_scr[...]
        m_new = jnp.maximum(m_old, jnp.max(s, axis=0, keepdims=True))
        pe = jnp.exp(s - m_new)
        alpha = jnp.exp(m_old - m_new)
        l_scr[...] = alpha * l_scr[...] + jnp.sum(pe, axis=0, keepdims=True)
        acc_scr[...] = alpha * acc_scr[...] + jnp.dot(vt_tile, pe.astype(MXU_DTYPE), preferred_element_type=F32)
        m_scr[...] = m_new

    def k_rows(ref, tile):
        return ref[0, pl.ds(pl.multiple_of(tile * Q_BLOCK, Q_BLOCK), Q_BLOCK), :]

    def v_cols(ref, tile):
        return ref[0, 0, :, pl.ds(pl.multiple_of(tile * Q_BLOCK, Q_BLOCK), Q_BLOCK)]

    def sel_rows(tile):
        half = Q_BLOCK // 2
        r0 = selb_scr[pl.ds(2 * tile, 1), :]
        r1 = selb_scr[pl.ds(2 * tile + 1, 1), :]
        return jnp.concatenate([jnp.broadcast_to(r0, (half, QL)), jnp.broadcast_to(r1, (half, QL))], axis=0)

    reset()
    step(k_rows(ks_ref, qb), v_cols(vst_ref, qb), sel_rows(qb) + bdiag_ref[0])

    @pl.when(qb >= 1)
    def _():
        step(k_rows(ks_ref, qb - 1), v_cols(vst_ref, qb - 1), sel_rows(qb - 1) + bprev_ref[0])

    def far_body(kt, carry):
        step(k_rows(ks_ref, kt), v_cols(vst_ref, kt), sel_rows(kt) + c31)
        return carry

    jax.lax.fori_loop(0, jnp.maximum(qb - 1, 0), far_body, 0)
    o_s = acc_scr[...] * (1.0 / l_scr[...])

    reset()
    step(k_rows(kw_ref, qb), v_cols(vwt_ref, qb), bdiag_ref[0])

    @pl.when(qb >= 1)
    def _():
        step(k_rows(kw_ref, qb - 1), v_cols(vwt_ref, qb - 1), bprev_ref[0])

    n_win = WINDOW // Q_BLOCK
    for back in range(2, n_win):
        @pl.when(qb >= back)
        def _(back=back):
            step(k_rows(kw_ref, qb - back), v_cols(vwt_ref, qb - back), jnp.broadcast_to(c31, (Q_BLOCK, QL)))

    @pl.when(qb >= n_win)
    def _():
        krow = jax.lax.broadcasted_iota(jnp.int32, (Q_BLOCK, QL), 0)
        icol = jax.lax.broadcasted_iota(jnp.int32, (Q_BLOCK, QL), 1) & (Q_BLOCK - 1)
        edge = jnp.where(krow > icol, jnp.broadcast_to(c31, (Q_BLOCK, QL)), NEG)
        step(k_rows(kw_ref, qb - n_win), v_cols(vwt_ref, qb - n_win), edge)

    o_w = acc_scr[...] * (1.0 / l_scr[...])

    gates = jax.nn.sigmoid(g_ref[0, 0, 0])
    o = gates[0:1, :] * o_c + gates[1:2, :] * o_s + gates[2:3, :] * o_w
    for r in range(HEADS_PER_GROUP):
        o_ref[0, :, r * HEAD_DIM:(r + 1) * HEAD_DIM] = o[:, r * Q_BLOCK:(r + 1) * Q_BLOCK].T.astype(o_ref.dtype)


def _nsa(qt, kc_pad, vct_pad, selt, ks_src, ks_col, vst, kw_src, kw_col, vwt, gt, bdiag, bprev, bc, c31, *, seq):
    b, g, nq = qt.shape[:3]
    ncp = kc_pad.shape[2]
    n_slc = selt.shape[0]
    top_k = min(SLC_TOPK, n_slc)
    per_g = lambda rows: pl.BlockSpec((1, rows, QL), lambda i, j, k: (j, 0, 0))
    return pl.pallas_call(
        functools.partial(_nsa_kernel, top_k=top_k),
        grid=(b, g, nq),
        in_specs=[pl.BlockSpec((1, 1, 1, HEAD_DIM, QL), lambda i, j, k: (i, j, k, 0, 0)),
                  pl.BlockSpec((1, 1, ncp, HEAD_DIM), lambda i, j, k: (i, j, 0, 0)),
                  pl.BlockSpec((1, 1, HEAD_DIM, ncp), lambda i, j, k: (i, j, 0, 0)),
                  pl.BlockSpec((n_slc, ncp), lambda i, j, k: (0, 0)),
                  pl.BlockSpec((1, seq, HEAD_DIM), lambda i, j, k: (i, 0, ks_col + j)),
                  pl.BlockSpec((1, 1, HEAD_DIM, seq), lambda i, j, k: (i, j, 0, 0)),
                  pl.BlockSpec((1, seq, HEAD_DIM), lambda i, j, k: (i, 0, kw_col + j)),
                  pl.BlockSpec((1, 1, HEAD_DIM, seq), lambda i, j, k: (i, j, 0, 0)),
                  pl.BlockSpec((1, 1, 1, 8, QL), lambda i, j, k: (i, j, k, 0, 0)),
                  per_g(Q_BLOCK), per_g(Q_BLOCK), per_g(CMP_NEAR), per_g(8)],
        out_specs=pl.BlockSpec((1, Q_BLOCK, QL), lambda i, j, k: (i, k, j)),
        out_shape=jax.ShapeDtypeStruct((b, seq, g * QL), MXU_DTYPE),
        scratch_shapes=[pltpu.VMEM((ncp, QL), F32), pltpu.VMEM((n_slc, QL), F32),
                        pltpu.VMEM((1, QL), F32), pltpu.VMEM((1, QL), F32), pltpu.VMEM((HEAD_DIM, QL), F32)],
        compiler_params=_cparams(("arbitrary", "arbitrary", "arbitrary")),
        name="nsa",
    )(qt, kc_pad, vct_pad, selt, ks_src, vst, kw_src, vwt, gt, bdiag, bprev, bc, c31)


def _sgu_kernel(zu_ref, zv_ref, lg_ref, lb_ref, w_ref, bs_ref, o_ref):
    u = jax.nn.gelu(zu_ref[0].astype(F32))
    v = jax.nn.gelu(zv_ref[0].astype(F32))
    mu = jnp.mean(v, axis=-1, keepdims=True)
    var = jnp.mean(jnp.square(v - mu), axis=-1, keepdims=True)
    vn = ((v - mu) * jax.lax.rsqrt(var + EPS) * lg_ref[...] + lb_ref[...]).astype(MXU_DTYPE)
    tc = w_ref.shape[1]
    causal = (jax.lax.broadcasted_iota(jnp.int32, (tc, tc), 0)
              >= jax.lax.broadcasted_iota(jnp.int32, (tc, tc), 1))
    bs = bs_ref[...]
    for h in range(w_ref.shape[0]):
        sl = slice(h * SGU_GROUP_DIM, (h + 1) * SGU_GROUP_DIM)
        w = jnp.where(causal, w_ref[h], 0.0).astype(MXU_DTYPE)
        mixed = jnp.dot(w, vn[:, sl], preferred_element_type=F32) + bs[:, h:h + 1]
        o_ref[0, :, sl] = (u[:, sl] * mixed).astype(o_ref.dtype)


def _sgu(proj3, ln_g, ln_b, w_s, b_st):
    b, t, _ = proj3.shape
    gs, tc, _ = w_s.shape
    vec = pl.BlockSpec((1, SGU_WIDTH), lambda i, j: (0, 0))
    return pl.pallas_call(
        _sgu_kernel,
        grid=(b, t // tc),
        in_specs=[pl.BlockSpec((1, tc, SGU_WIDTH), lambda i, j: (i, j, 0)),
                  pl.BlockSpec((1, tc, SGU_WIDTH), lambda i, j: (i, j, 1)),
                  vec, vec,
                  pl.BlockSpec((gs, tc, tc), lambda i, j: (0, 0, 0)),
                  pl.BlockSpec((tc, gs), lambda i, j: (0, 0))],
        out_specs=pl.BlockSpec((1, tc, SGU_WIDTH), lambda i, j: (i, j, 0)),
        out_shape=jax.ShapeDtypeStruct((b, t, SGU_WIDTH), MXU_DTYPE),
        compiler_params=_cparams(("arbitrary", "arbitrary")),
        name="sgu",
    )(proj3, proj3, ln_g, ln_b, w_s, b_st)


def _merge_kernel(ya_ref, yb_ref, wa_ref, wb_ref, ga_ref, gb_ref, o_ref):
    pa = jnp.dot(ya_ref[...], wa_ref[...], preferred_element_type=F32)
    pb = jnp.dot(yb_ref[...], wb_ref[...], preferred_element_type=F32)
    merged = (jax.nn.sigmoid(ga_ref[...].astype(F32)) * pa + jax.nn.sigmoid(gb_ref[...].astype(F32)) * pb)
    o_ref[...] = merged.astype(o_ref.dtype)


def _merge(ya, yb, wa, wb, proj, ga_col, gb_col, *, tm=1024, tn=512):
    m, ka = ya.shape
    kb = yb.shape[1]
    n = wa.shape[1]
    return pl.pallas_call(
        _merge_kernel,
        grid=(m // tm, n // tn),
        in_specs=[pl.BlockSpec((tm, ka), lambda i, j: (i, 0)),
                  pl.BlockSpec((tm, kb), lambda i, j: (i, 0)),
                  pl.BlockSpec((ka, tn), lambda i, j: (0, j)),
                  pl.BlockSpec((kb, tn), lambda i, j: (0, j)),
                  pl.BlockSpec((tm, tn), lambda i, j: (i, ga_col + j)),
                  pl.BlockSpec((tm, tn), lambda i, j: (i, gb_col + j))],
        out_specs=pl.BlockSpec((tm, tn), lambda i, j: (i, j)),
        out_shape=jax.ShapeDtypeStruct((m, n), MXU_DTYPE),
        compiler_params=_cparams(("arbitrary", "arbitrary")),
        name="merge",
    )(ya, yb, wa, wb, proj, proj)


def _ffn_up_kernel(h_ref, halo_ref, w_ref, cw_ref, cb_ref, o_ref, *, seq):
    i = pl.program_id(0)
    tm = h_ref.shape[0]
    w = w_ref[...]
    a = jnp.dot(h_ref[...], w, preferred_element_type=F32)
    ah = jnp.dot(halo_ref[...], w, preferred_element_type=F32)
    ah = jnp.where((i * tm) % seq == 0, 0.0, ah)
    row = jax.lax.broadcasted_iota(jnp.int32, a.shape, 0)
    a1 = jnp.where(row == 0, ah[7:8, :], pltpu.roll(a, 1, 0))
    a2 = jnp.where(row == 0, ah[6:7, :], jnp.where(row == 1, ah[7:8, :], pltpu.roll(a, 2, 0)))
    cw = cw_ref[...]
    conv = cb_ref[0:1, :] + a2 * cw[0:1, :] + a1 * cw[1:2, :] + a * cw[2:3, :]
    half = conv.shape[1] // 2
    gate = conv[:, :half]
    o_ref[...] = (gate * jax.nn.sigmoid(gate) * conv[:, half:]).astype(o_ref.dtype)


def _ffn_up(h, w, cw8, cb8, *, seq, tm=1024):
    m, d = h.shape
    n2 = w.shape[1]
    tn2 = 2 * FFN_HALF_TILE
    halo_blocks = tm // 8
    return pl.pallas_call(
        functools.partial(_ffn_up_kernel, seq=seq),
        grid=(m // tm, n2 // tn2),
        in_specs=[pl.BlockSpec((tm, d), lambda i, j: (i, 0)),
                  pl.BlockSpec((8, d), lambda i, j: (jnp.maximum(i * halo_blocks - 1, 0), 0)),
                  pl.BlockSpec((d, tn2), lambda i, j: (0, j)),
                  pl.BlockSpec((8, tn2), lambda i, j: (0, j)),
                  pl.BlockSpec((8, tn2), lambda i, j: (0, j))],
        out_specs=pl.BlockSpec((tm, FFN_HALF_TILE), lambda i, j: (i, j)),
        out_shape=jax.ShapeDtypeStruct((m, n2 // 2), MXU_DTYPE),
        compiler_params=_cparams(("arbitrary", "arbitrary")),
        name="ffn_up",
    )(h, h, w, cw8, cb8)


def _interleave_halves(a, tile):
    f = a.shape[-1] // 2
    lead = a.shape[:-1]
    first = a[..., :f].reshape(*lead, f // tile, 1, tile)
    second = a[..., f:].reshape(*lead, f // tile, 1, tile)
    return jnp.concatenate([first, second], axis=-2).reshape(*lead, 2 * f)


def _pad_rows(a, rows):
    return jnp.pad(a, ((0, rows - a.shape[0]), (0, 0)))


def _mixer_branches(h, w_in, cmp_pos, cmp_w1, cmp_b1, cmp_w2, cmp_b2, rel_table,
                    sgu_ln_g, sgu_ln_b, sgu_w, sgu_b):
    b, t, d = h.shape
    g, r, dk = NSA_KV_GROUPS, HEADS_PER_GROUP, HEAD_DIM
    m = b * t
    nq = t // Q_BLOCK
    n_cmp = (t - CMP_BLOCK) // CMP_STRIDE + 1
    n_slc = t // SLC_BLOCK
    rows16 = t // CMP_STRIDE
    ncp = rows16 + LANES

    o_q = NSA_WIDTH
    o_kv = [o_q + i * KV_WIDTH for i in range(6)]
    o_gn = o_q + 6 * KV_WIDTH
    o_z = o_gn + NSA_HEADS * 3
    o_ga = o_z + 2 * SGU_WIDTH
    o_gb = o_ga + d
    w_main = jnp.concatenate([w_in[:, o_z:o_ga], w_in[:, o_ga:o_gb], w_in[:, o_gb:o_gb + d],
                              w_in[:, :o_gn]], axis=1).astype(MXU_DTYPE)
    c_ga, c_gb, c_q = 2 * SGU_WIDTH, 2 * SGU_WIDTH + d, 2 * SGU_WIDTH + 2 * d
    c_kv = [c_q + NSA_WIDTH + i * KV_WIDTH for i in range(6)]
    n_main = w_main.shape[1]
    col_scale = jnp.ones((1, n_main), F32).at[:, c_q:c_q + NSA_WIDTH].set(HEAD_DIM ** -0.5)
    w_gn = jnp.pad(w_in[:, o_gn:o_z], ((0, 0), (0, LANES - NSA_HEADS * 3))).astype(MXU_DTYPE)

    h2d = h.reshape(m, d)
    proj = _matmul(h2d, w_main, tm=1024, tn=1024, tk=d, out_dtype=MXU_DTYPE, col_scale=col_scale, name="proj")
    gn = _matmul(h2d, w_gn, tm=1024, tn=LANES, tk=d, out_dtype=F32, name="proj_gates")
    proj3 = proj.reshape(b, t, n_main)

    kv_c = proj3[:, :, c_kv[0]:c_kv[2]].reshape(b, rows16, CMP_STRIDE, 2, g, dk)
    a_cmp = kv_c.transpose(0, 3, 4, 1, 2, 5).reshape(b, 2, g, rows16, CMP_STRIDE * dk)
    pos8 = jnp.broadcast_to(cmp_pos.reshape(2, 1, CMP_BLOCK * dk), (2, 8, CMP_BLOCK * dk))
    cmp_out = _compress(a_cmp, cmp_w1.astype(MXU_DTYPE), pos8, cmp_b1.reshape(2, 1, CMP_HIDDEN),
                        cmp_w2.astype(MXU_DTYPE), cmp_b2.reshape(2, 1, dk), n_cmp)
    cmp_pad = jnp.pad(cmp_out, ((0, 0), (0, 0), (0, 0), (CMP_FRONT, ncp - CMP_FRONT - rows16), (0, 0)))
    kc_pad = cmp_pad[:, 0]
    vct_pad = cmp_pad[:, 1].transpose(0, 1, 3, 2).astype(MXU_DTYPE)

    qt = proj3[:, :, c_q:c_q + NSA_WIDTH].reshape(b, nq, Q_BLOCK, g, r, dk)
    qt = qt.transpose(0, 3, 1, 5, 4, 2).reshape(b, g, nq, dk, QL)
    vst = proj3[:, :, c_kv[3]:c_kv[3] + KV_WIDTH].reshape(b, t, g, dk).transpose(0, 2, 3, 1)
    vwt = proj3[:, :, c_kv[5]:c_kv[5] + KV_WIDTH].reshape(b, t, g, dk).transpose(0, 2, 3, 1)
    gt = gn[:, :NSA_HEADS * 3].reshape(b, nq, Q_BLOCK, g, r, 3).transpose(0, 3, 1, 5, 4, 2).reshape(b, g, nq, 3, QL)
    gt = jnp.pad(gt, ((0, 0), (0, 0), (0, 0), (0, 5), (0, 0)))
    bdiag, bprev, bc, c31 = _bias_tiles(rel_table)
    selt = jnp.asarray(_selection_map_t(n_cmp, n_slc, ncp), MXU_DTYPE)
    y_a = _nsa(qt, kc_pad, vct_pad, selt, proj3, c_kv[2] // dk, vst, proj3, c_kv[4] // dk, vwt, gt,
               bdiag, bprev, bc, c31, seq=t)

    y_b = _sgu(proj3, sgu_ln_g.reshape(1, SGU_WIDTH), sgu_ln_b.reshape(1, SGU_WIDTH), sgu_w, sgu_b.T)
    return proj, y_a, y_b, c_ga, c_gb


def _token_mixer(h, w_in, cmp_pos, cmp_w1, cmp_b1, cmp_w2, cmp_b2, rel_table,
                 sgu_ln_g, sgu_ln_b, sgu_w, sgu_b, w_proj_nsa, w_proj_sgu):
    proj, y_a, y_b, c_ga, c_gb = _mixer_branches(h, w_in, cmp_pos, cmp_w1, cmp_b1, cmp_w2, cmp_b2, rel_table,
                                                 sgu_ln_g, sgu_ln_b, sgu_w, sgu_b)
    m = proj.shape[0]
    tn = 512
    return _merge(y_a.reshape(m, NSA_WIDTH), y_b.reshape(m, SGU_WIDTH),
                  w_proj_nsa.astype(MXU_DTYPE), w_proj_sgu.astype(MXU_DTYPE),
                  proj, c_ga // tn, c_gb // tn, tm=1024, tn=tn)


def kernel(x, c, w_mod, b_mod, g_norms, w_in, cmp_pos, cmp_w1, cmp_b1, cmp_w2, cmp_b2, rel_table, sgu_ln_g, sgu_ln_b, sgu_w, sgu_b, w_proj_nsa, w_proj_sgu, w_out, w_ffn_up, ffn_conv_w, ffn_conv_b, w_ffn_down):
    b, t, d = x.shape
    m = b * t
    depth = w_mod.shape[0]
    c8 = jnp.pad(c, ((0, 8 - b), (0, 0)))
    for l in range(depth):
        mod = _mod(c8, w_mod[l], b_mod[l].reshape(1, -1))[:b]
        sh1, sc1, gt1, sh2, sc2, gt2 = [v.reshape(b, 1, d) for v in jnp.split(mod, 6, axis=-1)]
        gn = g_norms[l]

        h = _norm_mod(x, gn[0:1], sh1, sc1)
        merged = _token_mixer(h, w_in[l], cmp_pos[l], cmp_w1[l], cmp_b1[l], cmp_w2[l], cmp_b2[l], rel_table,
                              sgu_ln_g[l], sgu_ln_b[l], sgu_w[l], sgu_b[l], w_proj_nsa[l], w_proj_sgu[l])
        y = _matmul(merged, w_out[l].astype(MXU_DTYPE), tm=1024, tn=1024, tk=d, out_dtype=F32, name="out_proj")
        x1, h2 = _resid_norm_mod(x, y.reshape(b, t, d), gn[1:2], gt1, gn[2:3], sh2, sc2)

        w_up = _interleave_halves(w_ffn_up[l], FFN_HALF_TILE).astype(MXU_DTYPE)
        cw8 = _pad_rows(_interleave_halves(ffn_conv_w[l], FFN_HALF_TILE), 8)
        cb8 = _pad_rows(_interleave_halves(ffn_conv_b[l].reshape(1, -1), FFN_HALF_TILE), 8)
        act = _ffn_up(h2.reshape(m, d), w_up, cw8, cb8, seq=t)
        d_ff = act.shape[1]
        y2 = _matmul(act, w_ffn_down[l].astype(MXU_DTYPE), tm=512, tn=1024, tk=d_ff // 2, out_dtype=F32,
                     name="ffn_down")
        x = _resid_norm(x1, y2.reshape(b, t, d), gn[3:4], gt2)
    return x
```

```python
import functools
import math

import jax
import jax.numpy as jnp
import numpy as np
from jax.experimental import pallas as pl
from jax.experimental.pallas import tpu as pltpu

NSA_HEADS = 16
NSA_KV_GROUPS = 4
HEADS_PER_GROUP = NSA_HEADS // NSA_KV_GROUPS
HEAD_DIM = 128
CMP_BLOCK = 32
CMP_STRIDE = 16
CMP_HIDDEN = 256
SLC_BLOCK = 64
SLC_TOPK = 16
WINDOW = 512
Q_BLOCK = 128
SGU_GROUPS = 16
SGU_GROUP_DIM = 128
SGU_CHUNK = 128
SGU_WIDTH = SGU_GROUPS * SGU_GROUP_DIM
REL_BUCKETS = 32
REL_MAX_DIST = 128
CONV_WIDTH = 3
EPS = 1e-6
NEG = -1e30
FORCE = 1e6

NSA_WIDTH = NSA_HEADS * HEAD_DIM
KV_WIDTH = NSA_KV_GROUPS * HEAD_DIM
LANES = 128
QL = HEADS_PER_GROUP * Q_BLOCK
CMP_FRONT = 16
CMP_NEAR = 32
FFN_HALF_TILE = 256
FAR_TILES = 4

MXU_DTYPE = jnp.bfloat16
F32 = jnp.float32
VMEM_LIMIT = 56 * 1024 * 1024


def _cparams(sem):
    return pltpu.CompilerParams(dimension_semantics=sem, vmem_limit_bytes=VMEM_LIMIT)


def _mod_kernel(c_ref, w_ref, b_ref, o_ref):
    c = c_ref[...]
    ca = c * jax.nn.sigmoid(c)
    o_ref[...] = jnp.dot(ca.astype(MXU_DTYPE), w_ref[...].astype(MXU_DTYPE),
                         preferred_element_type=F32) + b_ref[...]


def _mod(c8, w_mod, b_mod, tn=512):
    rows, d = c8.shape
    n = w_mod.shape[1]
    return pl.pallas_call(
        _mod_kernel,
        grid=(n // tn,),
        in_specs=[pl.BlockSpec((rows, d), lambda j: (0, 0)),
                  pl.BlockSpec((d, tn), lambda j: (0, j)),
                  pl.BlockSpec((1, tn), lambda j: (0, j))],
        out_specs=pl.BlockSpec((rows, tn), lambda j: (0, j)),
        out_shape=jax.ShapeDtypeStruct((rows, n), F32),
        compiler_params=_cparams(("arbitrary",)),
        name="mod",
    )(c8, w_mod, b_mod)


def _rms(x, g):
    return x * jax.lax.rsqrt(jnp.mean(x * x, axis=-1, keepdims=True) + EPS) * g


def _norm_mod_kernel(x_ref, g_ref, sh_ref, sc_ref, o_ref):
    y = _rms(x_ref[0], g_ref[...])
    o_ref[0] = (y * (1.0 + sc_ref[0]) + sh_ref[0]).astype(o_ref.dtype)


def _norm_mod(x, g, sh, sc, tm=256):
    b, t, d = x.shape
    row = pl.BlockSpec((1, 1, d), lambda i, j: (i, 0, 0))
    return pl.pallas_call(
        _norm_mod_kernel,
        grid=(b, t // tm),
        in_specs=[pl.BlockSpec((1, tm, d), lambda i, j: (i, j, 0)),
                  pl.BlockSpec((1, d), lambda i, j: (0, 0)), row, row],
        out_specs=pl.BlockSpec((1, tm, d), lambda i, j: (i, j, 0)),
        out_shape=jax.ShapeDtypeStruct((b, t, d), MXU_DTYPE),
        compiler_params=_cparams(("arbitrary", "arbitrary")),
        name="norm_mod",
    )(x, g, sh, sc)


def _resid_norm_mod_kernel(x_ref, y_ref, g1_ref, gt_ref, g2_ref, sh_ref, sc_ref, x1_ref, h_ref):
    x1 = x_ref[0] + gt_ref[0] * _rms(y_ref[0], g1_ref[...])
    x1_ref[0] = x1
    h_ref[0] = (_rms(x1, g2_ref[...]) * (1.0 + sc_ref[0]) + sh_ref[0]).astype(h_ref.dtype)


def _resid_norm_mod(x, y, g1, gt, g2, sh, sc, tm=256):
    b, t, d = x.shape
    row = pl.BlockSpec((1, 1, d), lambda i, j: (i, 0, 0))
    gspec = pl.BlockSpec((1, d), lambda i, j: (0, 0))
    blk = pl.BlockSpec((1, tm, d), lambda i, j: (i, j, 0))
    return pl.pallas_call(
        _resid_norm_mod_kernel,
        grid=(b, t // tm),
        in_specs=[blk, blk, gspec, row, gspec, row, row],
        out_specs=[blk, blk],
        out_shape=[jax.ShapeDtypeStruct((b, t, d), F32), jax.ShapeDtypeStruct((b, t, d), MXU_DTYPE)],
        compiler_params=_cparams(("arbitrary", "arbitrary")),
        name="resid_norm_mod",
    )(x, y, g1, gt, g2, sh, sc)


def _resid_norm_kernel(x_ref, y_ref, g_ref, gt_ref, o_ref):
    o_ref[0] = x_ref[0] + gt_ref[0] * _rms(y_ref[0], g_ref[...])


def _resid_norm(x, y, g, gt, tm=256):
    b, t, d = x.shape
    blk = pl.BlockSpec((1, tm, d), lambda i, j: (i, j, 0))
    return pl.pallas_call(
        _resid_norm_kernel,
        grid=(b, t // tm),
        in_specs=[blk, blk, pl.BlockSpec((1, d), lambda i, j: (0, 0)),
                  pl.BlockSpec((1, 1, d), lambda i, j: (i, 0, 0))],
        out_specs=blk,
        out_shape=jax.ShapeDtypeStruct((b, t, d), F32),
        compiler_params=_cparams(("arbitrary", "arbitrary")),
        name="resid_norm",
    )(x, y, g, gt)


def _mm_kernel(*refs, nk, scaled):
    if scaled:
        x_ref, w_ref, s_ref, o_ref = refs[:4]
        rest = refs[4:]
    else:
        x_ref, w_ref, o_ref = refs[:3]
        s_ref = None
        rest = refs[3:]

    def finish(acc):
        if scaled:
            acc = acc * s_ref[...]
        o_ref[...] = acc.astype(o_ref.dtype)

    part = jnp.dot(x_ref[...], w_ref[...], preferred_element_type=F32)
    if nk == 1:
        finish(part)
        return
    acc_ref, = rest
    k = pl.program_id(2)

    @pl.when(k == 0)
    def _():
        acc_ref[...] = jnp.zeros(acc_ref.shape, F32)

    acc_ref[...] += part

    @pl.when(k == nk - 1)
    def _():
        finish(acc_ref[...])


def _matmul(x, w, *, tm, tn, tk, out_dtype, col_scale=None, name="matmul"):
    m, kdim = x.shape
    n = w.shape[1]
    nk = kdim // tk
    in_specs = [pl.BlockSpec((tm, tk), lambda i, j, k: (i, k)),
                pl.BlockSpec((tk, tn), lambda i, j, k: (k, j))]
    args = [x, w]
    if col_scale is not None:
        in_specs.append(pl.BlockSpec((1, tn), lambda i, j, k: (0, j)))
        args.append(col_scale)
    return pl.pallas_call(
        functools.partial(_mm_kernel, nk=nk, scaled=col_scale is not None),
        grid=(m // tm, n // tn, nk),
        in_specs=in_specs,
        out_specs=pl.BlockSpec((tm, tn), lambda i, j, k: (i, j)),
        out_shape=jax.ShapeDtypeStruct((m, n), out_dtype),
        scratch_shapes=[] if nk == 1 else [pltpu.VMEM((tm, tn), F32)],
        compiler_params=_cparams(("arbitrary", "arbitrary", "arbitrary")),
        name=name,
    )(*args)


def _compress_kernel(a_ref, w1_ref, pos_ref, b1_ref, w2_ref, b2_ref, o_ref, *, n_cmp):
    a = a_ref[0, 0, 0]
    w1 = w1_ref[0]
    half = a.shape[1]
    top = jnp.dot(a, w1[:half], preferred_element_type=F32)
    bot = jnp.dot(a, w1[half:], preferred_element_type=F32)
    rows = a.shape[0]
    pre = top + pltpu.roll(bot, rows - 1, 0)
    pos_term = jnp.dot(pos_ref[0].astype(MXU_DTYPE), w1, preferred_element_type=F32)[0:1]
    hid = jax.nn.gelu(pre + pos_term + b1_ref[0])
    out = jnp.dot(hid.astype(MXU_DTYPE), w2_ref[0], preferred_element_type=F32) + b2_ref[0]
    ridx = jax.lax.broadcasted_iota(jnp.int32, out.shape, 0)
    o_ref[0, 0, 0] = jnp.where(ridx < n_cmp, out, 0.0)


def _compress(a, w1, pos8, b1, w2, b2, n_cmp):
    b, two, g, rows, width = a.shape
    hid = w1.shape[2]
    dk = w2.shape[2]
    return pl.pallas_call(
        functools.partial(_compress_kernel, n_cmp=n_cmp),
        grid=(two, b, g),
        in_specs=[pl.BlockSpec((1, 1, 1, rows, width), lambda s, i, j: (i, s, j, 0, 0)),
                  pl.BlockSpec((1, 2 * width, hid), lambda s, i, j: (s, 0, 0)),
                  pl.BlockSpec((1, 8, 2 * width), lambda s, i, j: (s, 0, 0)),
                  pl.BlockSpec((1, 1, hid), lambda s, i, j: (s, 0, 0)),
                  pl.BlockSpec((1, hid, dk), lambda s, i, j: (s, 0, 0)),
                  pl.BlockSpec((1, 1, dk), lambda s, i, j: (s, 0, 0))],
        out_specs=pl.BlockSpec((1, 1, 1, rows, dk), lambda s, i, j: (i, s, j, 0, 0)),
        out_shape=jax.ShapeDtypeStruct((b, two, g, rows, dk), F32),
        compiler_params=_cparams(("arbitrary", "arbitrary", "arbitrary")),
        name="compress",
    )(a, w1, pos8, b1, w2, b2)


def _rel_bucket_np(dist):
    n = np.maximum(dist, 0)
    max_exact = REL_BUCKETS // 2
    nf = np.maximum(n, 1).astype(np.float32)
    large = max_exact + (np.log(nf / max_exact) / math.log(REL_MAX_DIST / max_exact)
                         * (REL_BUCKETS - max_exact)).astype(np.int32)
    large = np.minimum(large, REL_BUCKETS - 1)
    return np.where(n < max_exact, n, large).astype(np.int32)


def _bias_index_tiles():
    w = np.arange(WINDOW + Q_BLOCK)[:, None]
    i = np.arange(Q_BLOCK)[None, :]
    d_w = i - w + WINDOW
    win = np.where((d_w >= 0) & (d_w < WINDOW), _rel_bucket_np(d_w), -1)
    m = np.arange(CMP_NEAR)[:, None]
    d_c = i - CMP_STRIDE * m + (CMP_STRIDE * CMP_FRONT - CMP_BLOCK + 1)
    near = np.where(d_c >= 0, _rel_bucket_np(d_c), -1)
    return win.astype(np.int32), near.astype(np.int32)


def _bias_kernel(tab_ref, idw_ref, idc_ref, ow_ref, oc_ref, o31_ref):
    h = pl.program_id(0)

    def lut(idx):
        out = jnp.full(idx.shape, NEG, F32)
        for b in range(REL_BUCKETS):
            out = jnp.where(idx == b, tab_ref[b, h], out)
        return out

    ow_ref[0] = lut(idw_ref[...])
    oc_ref[0] = lut(idc_ref[...])
    o31_ref[0] = jnp.full(o31_ref.shape[1:], tab_ref[REL_BUCKETS - 1, h], F32)


def _bias_tiles(rel_table):
    win, near = _bias_index_tiles()
    g, r = NSA_KV_GROUPS, HEADS_PER_GROUP
    full = lambda a: pl.BlockSpec(a.shape, lambda h: (0, 0))
    head = lambda rows: pl.BlockSpec((1, rows, Q_BLOCK), lambda h: (h // r, 0, h % r))
    return pl.pallas_call(
        _bias_kernel,
        grid=(NSA_HEADS,),
        in_specs=[pl.BlockSpec(memory_space=pltpu.SMEM), full(win), full(near)],
        out_specs=[head(win.shape[0]), head(CMP_NEAR), head(8)],
        out_shape=[jax.ShapeDtypeStruct((g, win.shape[0], QL), F32),
                   jax.ShapeDtypeStruct((g, CMP_NEAR, QL), F32), jax.ShapeDtypeStruct((g, 8, QL), F32)],
        compiler_params=_cparams(("arbitrary",)),
        name="bias_tiles",
    )(rel_table, jnp.asarray(win), jnp.asarray(near))


def _selection_map_t(n_cmp, n_slc, ncp):
    ratio = SLC_BLOCK // CMP_STRIDE
    span = CMP_BLOCK // CMP_STRIDE
    d = np.arange(n_cmp)[:, None] - ratio * np.arange(n_slc)[None, :]
    a = d[..., None] + np.arange(span)
    sel = np.sum((a >= 0) & (a < ratio), axis=-1).astype(np.float32)
    out = np.zeros((n_slc, ncp), np.float32)
    out[:, CMP_FRONT:CMP_FRONT + n_cmp] = sel.T
    return out


def _nsa_kernel(q_ref, kc_ref, vct_ref, selt_ref, ks_ref, vst_ref, kw_ref, vwt_ref, g_ref,
                bnear_ref, bwin_ref, bc_ref, c31_ref, o_ref,
                p_scr, selb_scr, selbf_scr, m_scr, l_scr, acc_scr, *, top_k):
    qb = pl.program_id(2)
    q = q_ref[0, 0, 0]
    ncp = kc_ref.shape[2]
    n_slc = selt_ref.shape[0]
    c31 = c31_ref[0][0:1, :]
    lane_i = jax.lax.broadcasted_iota(jnp.int32, (1, QL), 1) & (Q_BLOCK - 1)
    t0 = qb * Q_BLOCK

    s_all = jnp.dot(kc_ref[0, 0].astype(MXU_DTYPE), q, preferred_element_type=F32)
    prow = jax.lax.broadcasted_iota(jnp.int32, (ncp, QL), 0)
    near0 = pl.multiple_of(qb * (Q_BLOCK // CMP_STRIDE), 8)
    s_far = jnp.where(prow < near0, jnp.where(prow >= CMP_FRONT, s_all + c31, NEG), NEG)
    kn = kc_ref[0, 0, pl.ds(near0, CMP_NEAR), :]
    mrow = jax.lax.broadcasted_iota(jnp.int32, (CMP_NEAR, QL), 0)
    s_near = jnp.dot(kn.astype(MXU_DTYPE), q, preferred_element_type=F32) + bc_ref[0]
    s_near = jnp.where(mrow >= CMP_FRONT - near0, s_near, NEG)
    mx = jnp.maximum(jnp.max(s_far, axis=0, keepdims=True), jnp.max(s_near, axis=0, keepdims=True))
    e_far = jnp.exp(s_far - mx)
    e_near = jnp.exp(s_near - mx)
    den = jnp.sum(e_far, axis=0, keepdims=True) + jnp.sum(e_near, axis=0, keepdims=True)
    inv = jnp.where(t0 + lane_i >= CMP_BLOCK - 1, 1.0 / den, 0.0)
    p_scr[...] = e_far * inv
    p_scr[pl.ds(near0, CMP_NEAR), :] = e_near * inv
    p = p_scr[...]
    o_c = jnp.dot(vct_ref[0, 0], p.astype(MXU_DTYPE), preferred_element_type=F32)

    psum = p[:, 0:Q_BLOCK]
    for r in range(1, HEADS_PER_GROUP):
        psum = psum + p[:, r * Q_BLOCK:(r + 1) * Q_BLOCK]
    p_hi = psum.astype(MXU_DTYPE)
    p_lo = (psum - p_hi.astype(F32)).astype(MXU_DTYPE)
    selt = selt_ref[...]
    imp = (jnp.dot(selt, p_hi, preferred_element_type=F32)
           + jnp.dot(selt, p_lo, preferred_element_type=F32))
    jidx = jax.lax.broadcasted_iota(jnp.int32, (n_slc, Q_BLOCK), 0).astype(F32)
    iidx = jax.lax.broadcasted_iota(jnp.int32, (n_slc, Q_BLOCK), 1)
    cur = (qb * (Q_BLOCK // SLC_BLOCK) + jax.lax.shift_right_logical(iidx, int(math.log2(SLC_BLOCK)))).astype(F32)
    vis = jidx <= cur
    forced = jnp.where(jidx == 0.0, 1.0, jnp.where(jidx == cur, 1.0, jnp.where(jidx == cur - 1.0, 1.0, 0.0)))
    score = jnp.where(vis, jnp.where(forced > 0.5, FORCE, imp), -1.0)
    rem = score
    chosen = jnp.zeros((n_slc, Q_BLOCK), F32)
    for _ in range(top_k):
        best = jnp.max(rem, axis=0, keepdims=True)
        first = jnp.min(jnp.where(rem == best, jidx, float(n_slc)), axis=0, keepdims=True)
        hit = jidx == first
        chosen = jnp.where(hit, 1.0, chosen)
        rem = jnp.where(hit, -2.0, rem)
    selb = jnp.where(vis, jnp.where(chosen > 0.5, 0.0, NEG), NEG)
    selb4 = jnp.concatenate([selb] * HEADS_PER_GROUP, axis=1)
    selb_scr[...] = selb4
    selbf_scr[...] = selb4 + c31

    def first_chunk(k_rows, vt_cols, bias):
        s = jnp.dot(k_rows, q, preferred_element_type=F32) + bias
        m_new = jnp.max(s, axis=0, keepdims=True)
        pe = jnp.exp(s - m_new)
        m_scr[...] = m_new
        l_scr[...] = jnp.sum(pe, axis=0, keepdims=True)
        acc_scr[...] = jnp.dot(vt_cols, pe.astype(MXU_DTYPE), preferred_element_type=F32)

    def next_chunk(k_rows, vt_cols, bias):
        s = jnp.dot(k_rows, q, preferred_element_type=F32) + bias
        m_old = m_scr[...]
        m_new = jnp.maximum(m_old, jnp.max(s, axis=0, keepdims=True))
        pe = jnp.exp(s - m_new)
        alpha = jnp.exp(m_old - m_new)
        l_scr[...] = alpha * l_scr[...] + jnp.sum(pe, axis=0, keepdims=True)
        acc_scr[...] = alpha * acc_scr[...] + jnp.dot(vt_cols, pe.astype(MXU_DTYPE), preferred_element_type=F32)
        m_scr[...] = m_new

    def keys(ref, tile, n_tiles):
        return ref[0, pl.ds(pl.multiple_of(tile * Q_BLOCK, Q_BLOCK), n_tiles * Q_BLOCK), :]

    def values_t(ref, tile, n_tiles):
        return ref[0, 0, :, pl.ds(pl.multiple_of(tile * Q_BLOCK, Q_BLOCK), n_tiles * Q_BLOCK)]

    def block_mask(scr, tile, n_tiles):
        blocks_per_tile = Q_BLOCK // SLC_BLOCK
        rows = [jnp.broadcast_to(scr[pl.ds(blocks_per_tile * tile + u, 1), :], (SLC_BLOCK, QL))
                for u in range(blocks_per_tile * n_tiles)]
        return jnp.concatenate(rows, axis=0)

    @pl.when(qb >= 1)
    def _():
        first_chunk(keys(ks_ref, qb - 1, 2), values_t(vst_ref, qb - 1, 2),
                    block_mask(selb_scr, qb - 1, 2) + bnear_ref[0])

    @pl.when(qb == 0)
    def _():
        first_chunk(keys(ks_ref, qb, 1), values_t(vst_ref, qb, 1),
                    block_mask(selb_scr, qb, 1) + bnear_ref[0, Q_BLOCK:2 * Q_BLOCK, :])

    n_far = jnp.maximum(qb - 1, 0)
    n_big = n_far // FAR_TILES

    def far_chunk(c, carry):
        next_chunk(keys(ks_ref, c * FAR_TILES, FAR_TILES), values_t(vst_ref, c * FAR_TILES, FAR_TILES),
                   block_mask(selbf_scr, c * FAR_TILES, FAR_TILES))
        return carry

    def far_tile(kt, carry):
        next_chunk(keys(ks_ref, kt, 1), values_t(vst_ref, kt, 1), block_mask(selbf_scr, kt, 1))
        return carry

    jax.lax.fori_loop(0, n_big, far_chunk, 0)
    jax.lax.fori_loop(n_big * FAR_TILES, n_far, far_tile, 0)
    o_s = acc_scr[...] * (1.0 / l_scr[...])

    n_win = WINDOW // Q_BLOCK

    @pl.when(qb >= n_win)
    def _():
        first_chunk(keys(kw_ref, qb - n_win, n_win + 1), values_t(vwt_ref, qb - n_win, n_win + 1), bwin_ref[0])

    @pl.when(qb < n_win)
    def _():
        first_chunk(keys(kw_ref, qb, 1), values_t(vwt_ref, qb, 1), bwin_ref[0, n_win * Q_BLOCK:, :])
        for back in range(1, n_win):
            @pl.when(qb >= back)
            def _(back=back):
                lo = (n_win - back) * Q_BLOCK
                next_chunk(keys(kw_ref, qb - back, 1), values_t(vwt_ref, qb - back, 1),
                           bwin_ref[0, lo:lo + Q_BLOCK, :])

    o_w = acc_scr[...] * (1.0 / l_scr[...])

    gates = jax.nn.sigmoid(g_ref[0, 0, 0])
    o = gates[0:1, :] * o_c + gates[1:2, :] * o_s + gates[2:3, :] * o_w
    for r in range(HEADS_PER_GROUP):
        o_ref[0, :, r * HEAD_DIM:(r + 1) * HEAD_DIM] = o[:, r * Q_BLOCK:(r + 1) * Q_BLOCK].T.astype(o_ref.dtype)


def _nsa(qt, kc_pad, vct_pad, selt, ks_src, ks_col, vst, kw_src, kw_col, vwt, gt, bnear, bwin, bc, c31, *, seq):
    b, g, nq = qt.shape[:3]
    ncp = kc_pad.shape[2]
    n_slc = selt.shape[0]
    top_k = min(SLC_TOPK, n_slc)
    per_g = lambda rows: pl.BlockSpec((1, rows, QL), lambda i, j, k: (j, 0, 0))
    n_near, n_winrows = bnear.shape[1], bwin.shape[1]
    return pl.pallas_call(
        functools.partial(_nsa_kernel, top_k=top_k),
        grid=(b, g, nq),
        in_specs=[pl.BlockSpec((1, 1, 1, HEAD_DIM, QL), lambda i, j, k: (i, j, k, 0, 0)),
                  pl.BlockSpec((1, 1, ncp, HEAD_DIM), lambda i, j, k: (i, j, 0, 0)),
                  pl.BlockSpec((1, 1, HEAD_DIM, ncp), lambda i, j, k: (i, j, 0, 0)),
                  pl.BlockSpec((n_slc, ncp), lambda i, j, k: (0, 0)),
                  pl.BlockSpec((1, seq, HEAD_DIM), lambda i, j, k: (i, 0, ks_col + j)),
                  pl.BlockSpec((1, 1, HEAD_DIM, seq), lambda i, j, k: (i, j, 0, 0)),
                  pl.BlockSpec((1, seq, HEAD_DIM), lambda i, j, k: (i, 0, kw_col + j)),
                  pl.BlockSpec((1, 1, HEAD_DIM, seq), lambda i, j, k: (i, j, 0, 0)),
                  pl.BlockSpec((1, 1, 1, 8, QL), lambda i, j, k: (i, j, k, 0, 0)),
                  per_g(n_near), per_g(n_winrows), per_g(CMP_NEAR), per_g(8)],
        out_specs=pl.BlockSpec((1, Q_BLOCK, QL), lambda i, j, k: (i, k, j)),
        out_shape=jax.ShapeDtypeStruct((b, seq, g * QL), MXU_DTYPE),
        scratch_shapes=[pltpu.VMEM((ncp, QL), F32), pltpu.VMEM((n_slc, QL), F32), pltpu.VMEM((n_slc, QL), F32),
                        pltpu.VMEM((1, QL), F32), pltpu.VMEM((1, QL), F32), pltpu.VMEM((HEAD_DIM, QL), F32)],
        compiler_params=_cparams(("arbitrary", "arbitrary", "arbitrary")),
        name="nsa",
    )(qt, kc_pad, vct_pad, selt, ks_src, vst, kw_src, vwt, gt, bnear, bwin, bc, c31)


def _sgu_kernel(zu_ref, zv_ref, lg_ref, lb_ref, w_ref, bs_ref, o_ref):
    u = jax.nn.gelu(zu_ref[0].astype(F32))
    v = jax.nn.gelu(zv_ref[0].astype(F32))
    mu = jnp.mean(v, axis=-1, keepdims=True)
    var = jnp.mean(jnp.square(v - mu), axis=-1, keepdims=True)
    vn = ((v - mu) * jax.lax.rsqrt(var + EPS) * lg_ref[...] + lb_ref[...]).astype(MXU_DTYPE)
    tc = w_ref.shape[1]
    causal = (jax.lax.broadcasted_iota(jnp.int32, (tc, tc), 0)
              >= jax.lax.broadcasted_iota(jnp.int32, (tc, tc), 1))
    bs = bs_ref[...]
    for h in range(w_ref.shape[0]):
        sl = slice(h * SGU_GROUP_DIM, (h + 1) * SGU_GROUP_DIM)
        w = jnp.where(causal, w_ref[h], 0.0).astype(MXU_DTYPE)
        mixed = jnp.dot(w, vn[:, sl], preferred_element_type=F32) + bs[:, h:h + 1]
        o_ref[0, :, sl] = (u[:, sl] * mixed).astype(o_ref.dtype)


def _sgu(proj3, ln_g, ln_b, w_s, b_st):
    b, t, _ = proj3.shape
    gs, tc, _ = w_s.shape
    vec = pl.BlockSpec((1, SGU_WIDTH), lambda i, j: (0, 0))
    return pl.pallas_call(
        _sgu_kernel,
        grid=(b, t // tc),
        in_specs=[pl.BlockSpec((1, tc, SGU_WIDTH), lambda i, j: (i, j, 0)),
                  pl.BlockSpec((1, tc, SGU_WIDTH), lambda i, j: (i, j, 1)),
                  vec, vec,
                  pl.BlockSpec((gs, tc, tc), lambda i, j: (0, 0, 0)),
                  pl.BlockSpec((tc, gs), lambda i, j: (0, 0))],
        out_specs=pl.BlockSpec((1, tc, SGU_WIDTH), lambda i, j: (i, j, 0)),
        out_shape=jax.ShapeDtypeStruct((b, t, SGU_WIDTH), MXU_DTYPE),
        compiler_params=_cparams(("arbitrary", "arbitrary")),
        name="sgu",
    )(proj3, proj3, ln_g, ln_b, w_s, b_st)


def _merge_kernel(ya_ref, yb_ref, wa_ref, wb_ref, ga_ref, gb_ref, o_ref):
    pa = jnp.dot(ya_ref[...], wa_ref[...], preferred_element_type=F32)
    pb = jnp.dot(yb_ref[...], wb_ref[...], preferred_element_type=F32)
    merged = (jax.nn.sigmoid(ga_ref[...].astype(F32)) * pa + jax.nn.sigmoid(gb_ref[...].astype(F32)) * pb)
    o_ref[...] = merged.astype(o_ref.dtype)


def _merge(ya, yb, wa, wb, proj, ga_col, gb_col, *, tm=1024, tn=512):
    m, ka = ya.shape
    kb = yb.shape[1]
    n = wa.shape[1]
    return pl.pallas_call(
        _merge_kernel,
        grid=(m // tm, n // tn),
        in_specs=[pl.BlockSpec((tm, ka), lambda i, j: (i, 0)),
                  pl.BlockSpec((tm, kb), lambda i, j: (i, 0)),
                  pl.BlockSpec((ka, tn), lambda i, j: (0, j)),
                  pl.BlockSpec((kb, tn), lambda i, j: (0, j)),
                  pl.BlockSpec((tm, tn), lambda i, j: (i, ga_col + j)),
                  pl.BlockSpec((tm, tn), lambda i, j: (i, gb_col + j))],
        out_specs=pl.BlockSpec((tm, tn), lambda i, j: (i, j)),
        out_shape=jax.ShapeDtypeStruct((m, n), MXU_DTYPE),
        compiler_params=_cparams(("arbitrary", "arbitrary")),
        name="merge",
    )(ya, yb, wa, wb, proj, proj)


def _ffn_up_kernel(h_ref, halo_ref, wg_ref, wu_ref, cwg_ref, cwu_ref, cbg_ref, cbu_ref, o_ref, *, seq, sub):
    i = pl.program_id(0)
    tm = h_ref.shape[0]
    wg = wg_ref[...]
    wu = wu_ref[...]
    row = jax.lax.broadcasted_iota(jnp.int32, (sub, wg.shape[1]), 0)

    def conv(a, before, cw_ref, cb_ref):
        a1 = jnp.where(row == 0, before[7:8, :], pltpu.roll(a, 1, 0))
        a2 = jnp.where(row == 0, before[6:7, :], jnp.where(row == 1, before[7:8, :], pltpu.roll(a, 2, 0)))
        return cb_ref[0:1, :] + a2 * cw_ref[0:1, :] + a1 * cw_ref[1:2, :] + a * cw_ref[2:3, :]

    keep = jnp.where((i * tm) % seq == 0, 0.0, 1.0)
    before_g = jnp.dot(halo_ref[...], wg, preferred_element_type=F32) * keep
    before_u = jnp.dot(halo_ref[...], wu, preferred_element_type=F32) * keep
    for s in range(tm // sub):
        hs = h_ref[s * sub:(s + 1) * sub, :]
        ag = jnp.dot(hs, wg, preferred_element_type=F32)
        au = jnp.dot(hs, wu, preferred_element_type=F32)
        gate = conv(ag, before_g, cwg_ref, cbg_ref)
        up = conv(au, before_u, cwu_ref, cbu_ref)
        o_ref[s * sub:(s + 1) * sub, :] = (gate * jax.nn.sigmoid(gate) * up).astype(o_ref.dtype)
        before_g = ag[sub - 8:, :]
        before_u = au[sub - 8:, :]


def _ffn_up(h, w, cw8, cb8, *, seq, tm=2048, sub=512):
    m, d = h.shape
    f = w.shape[1] // 2
    tn = FFN_HALF_TILE
    nj = f // tn
    halo_blocks = tm // 8
    gate_cols = lambda rows: pl.BlockSpec((rows, tn), lambda i, j: (0, j))
    up_cols = lambda rows: pl.BlockSpec((rows, tn), lambda i, j: (0, nj + j))
    return pl.pallas_call(
        functools.partial(_ffn_up_kernel, seq=seq, sub=sub),
        grid=(m // tm, nj),
        in_specs=[pl.BlockSpec((tm, d), lambda i, j: (i, 0)),
                  pl.BlockSpec((8, d), lambda i, j: (jnp.maximum(i * halo_blocks - 1, 0), 0)),
                  gate_cols(d), up_cols(d), gate_cols(8), up_cols(8), gate_cols(8), up_cols(8)],
        out_specs=pl.BlockSpec((tm, tn), lambda i, j: (i, j)),
        out_shape=jax.ShapeDtypeStruct((m, f), MXU_DTYPE),
        compiler_params=_cparams(("arbitrary", "arbitrary")),
        name="ffn_up",
    )(h, h, w, w, cw8, cw8, cb8, cb8)


def _pad_rows(a, rows):
    return jnp.pad(a, ((0, rows - a.shape[0]), (0, 0)))


def _mixer_branches(h, w_in, cmp_pos, cmp_w1, cmp_b1, cmp_w2, cmp_b2, rel_table,
                    sgu_ln_g, sgu_ln_b, sgu_w, sgu_b):
    b, t, d = h.shape
    g, r, dk = NSA_KV_GROUPS, HEADS_PER_GROUP, HEAD_DIM
    m = b * t
    nq = t // Q_BLOCK
    n_cmp = (t - CMP_BLOCK) // CMP_STRIDE + 1
    n_slc = t // SLC_BLOCK
    rows16 = t // CMP_STRIDE
    ncp = rows16 + LANES

    o_gn = NSA_WIDTH + 6 * KV_WIDTH
    o_z = o_gn + NSA_HEADS * 3
    w_att = w_in[:, :o_gn].astype(MXU_DTYPE)
    w_gn = jnp.pad(w_in[:, o_gn:o_z], ((0, 0), (0, LANES - NSA_HEADS * 3))).astype(MXU_DTYPE)
    w_rest = w_in[:, o_z:].astype(MXU_DTYPE)
    c_ga, c_gb = 2 * SGU_WIDTH, 2 * SGU_WIDTH + d
    c_kv = [NSA_WIDTH + i * KV_WIDTH for i in range(6)]
    col_scale = jnp.ones((1, o_gn), F32).at[:, :NSA_WIDTH].set(HEAD_DIM ** -0.5)

    h2d = h.reshape(m, d)
    proj = _matmul(h2d, w_rest, tm=1024, tn=1024, tk=d, out_dtype=MXU_DTYPE, name="proj_rest")
    patt = _matmul(h2d, w_att, tm=1024, tn=1024, tk=d, out_dtype=MXU_DTYPE, col_scale=col_scale, name="proj_att")
    gn = _matmul(h2d, w_gn, tm=1024, tn=LANES, tk=d, out_dtype=F32, name="proj_gates")
    proj3 = proj.reshape(b, t, proj.shape[1])
    patt3 = patt.reshape(b, t, o_gn)

    kv_c = patt3[:, :, c_kv[0]:c_kv[2]].reshape(b, rows16, CMP_STRIDE, 2, g, dk)
    a_cmp = kv_c.transpose(0, 3, 4, 1, 2, 5).reshape(b, 2, g, rows16, CMP_STRIDE * dk)
    pos8 = jnp.broadcast_to(cmp_pos.reshape(2, 1, CMP_BLOCK * dk), (2, 8, CMP_BLOCK * dk))
    cmp_out = _compress(a_cmp, cmp_w1.astype(MXU_DTYPE), pos8, cmp_b1.reshape(2, 1, CMP_HIDDEN),
                        cmp_w2.astype(MXU_DTYPE), cmp_b2.reshape(2, 1, dk), n_cmp)
    cmp_pad = jnp.pad(cmp_out, ((0, 0), (0, 0), (0, 0), (CMP_FRONT, ncp - CMP_FRONT - rows16), (0, 0)))
    kc_pad = cmp_pad[:, 0]
    vct_pad = cmp_pad[:, 1].transpose(0, 1, 3, 2).astype(MXU_DTYPE)

    qt = patt3[:, :, :NSA_WIDTH].reshape(b, nq, Q_BLOCK, g, r, dk)
    qt = qt.transpose(0, 3, 1, 5, 4, 2).reshape(b, g, nq, dk, QL)
    vst = patt3[:, :, c_kv[3]:c_kv[3] + KV_WIDTH].reshape(b, t, g, dk).transpose(0, 2, 3, 1)
    vwt = patt3[:, :, c_kv[5]:c_kv[5] + KV_WIDTH].reshape(b, t, g, dk).transpose(0, 2, 3, 1)
    gt = gn[:, :NSA_HEADS * 3].reshape(b, nq, Q_BLOCK, g, r, 3).transpose(0, 3, 1, 5, 4, 2).reshape(b, g, nq, 3, QL)
    gt = jnp.pad(gt, ((0, 0), (0, 0), (0, 0), (0, 5), (0, 0)))
    bwin, bc, c31 = _bias_tiles(rel_table)
    bnear = bwin[:, WINDOW - Q_BLOCK:, :]
    selt = jnp.asarray(_selection_map_t(n_cmp, n_slc, ncp), MXU_DTYPE)
    y_a = _nsa(qt, kc_pad, vct_pad, selt, patt3, c_kv[2] // dk, vst, patt3, c_kv[4] // dk, vwt, gt,
               bnear, bwin, bc, c31, seq=t)

    y_b = _sgu(proj3, sgu_ln_g.reshape(1, SGU_WIDTH), sgu_ln_b.reshape(1, SGU_WIDTH), sgu_w, sgu_b.T)
    return proj, y_a, y_b, c_ga, c_gb


def _token_mixer(h, w_in, cmp_pos, cmp_w1, cmp_b1, cmp_w2, cmp_b2, rel_table,
                 sgu_ln_g, sgu_ln_b, sgu_w, sgu_b, w_proj_nsa, w_proj_sgu):
    proj, y_a, y_b, c_ga, c_gb = _mixer_branches(h, w_in, cmp_pos, cmp_w1, cmp_b1, cmp_w2, cmp_b2, rel_table,
                                                 sgu_ln_g, sgu_ln_b, sgu_w, sgu_b)
    m = proj.shape[0]
    tn = 512
    return _merge(y_a.reshape(m, NSA_WIDTH), y_b.reshape(m, SGU_WIDTH),
                  w_proj_nsa.astype(MXU_DTYPE), w_proj_sgu.astype(MXU_DTYPE),
                  proj, c_ga // tn, c_gb // tn, tm=1024, tn=tn)


def kernel(x, c, w_mod, b_mod, g_norms, w_in, cmp_pos, cmp_w1, cmp_b1, cmp_w2, cmp_b2, rel_table, sgu_ln_g, sgu_ln_b, sgu_w, sgu_b, w_proj_nsa, w_proj_sgu, w_out, w_ffn_up, ffn_conv_w, ffn_conv_b, w_ffn_down):
    b, t, d = x.shape
    m = b * t
    depth = w_mod.shape[0]
    c8 = jnp.pad(c, ((0, 8 - b), (0, 0)))
    for l in range(depth):
        mod = _mod(c8, w_mod[l], b_mod[l].reshape(1, -1))[:b]
        sh1, sc1, gt1, sh2, sc2, gt2 = [v.reshape(b, 1, d) for v in jnp.split(mod, 6, axis=-1)]
        gn = g_norms[l]

        h = _norm_mod(x, gn[0:1], sh1, sc1)
        merged = _token_mixer(h, w_in[l], cmp_pos[l], cmp_w1[l], cmp_b1[l], cmp_w2[l], cmp_b2[l], rel_table,
                              sgu_ln_g[l], sgu_ln_b[l], sgu_w[l], sgu_b[l], w_proj_nsa[l], w_proj_sgu[l])
        y = _matmul(merged, w_out[l].astype(MXU_DTYPE), tm=1024, tn=1024, tk=d, out_dtype=F32, name="out_proj")
        x1, h2 = _resid_norm_mod(x, y.reshape(b, t, d), gn[1:2], gt1, gn[2:3], sh2, sc2)

        cw8 = _pad_rows(ffn_conv_w[l], 8)
        cb8 = _pad_rows(ffn_conv_b[l].reshape(1, -1), 8)
        act = _ffn_up(h2.reshape(m, d), w_ffn_up[l].astype(MXU_DTYPE), cw8, cb8, seq=t)
        d_ff = act.shape[1]
        y2 = _matmul(act, w_ffn_down[l].astype(MXU_DTYPE), tm=512, tn=512, tk=d_ff, out_dtype=F32, name="ffn_down")
        x = _resid_norm(x1, y2.reshape(b, t, d), gn[3:4], gt2)
    return x
```

```python
import functools
import math

import jax
import jax.numpy as jnp
import numpy as np
from jax.experimental import pallas as pl
from jax.experimental.pallas import tpu as pltpu

NSA_HEADS = 16
NSA_KV_GROUPS = 4
HEADS_PER_GROUP = NSA_HEADS // NSA_KV_GROUPS
HEAD_DIM = 128
CMP_BLOCK = 32
CMP_STRIDE = 16
CMP_HIDDEN = 256
SLC_BLOCK = 64
SLC_TOPK = 16
WINDOW = 512
Q_BLOCK = 128
SGU_GROUPS = 16
SGU_GROUP_DIM = 128
SGU_CHUNK = 128
SGU_WIDTH = SGU_GROUPS * SGU_GROUP_DIM
REL_BUCKETS = 32
REL_MAX_DIST = 128
CONV_WIDTH = 3
EPS = 1e-6
NEG = -1e30
FORCE = 1e6
LOG2E = math.log2(math.e)

NSA_WIDTH = NSA_HEADS * HEAD_DIM
KV_WIDTH = NSA_KV_GROUPS * HEAD_DIM
LANES = 128
QL = HEADS_PER_GROUP * Q_BLOCK
CMP_FRONT = 16
CMP_NEAR = 32
FFN_HALF_TILE = 256
FAR_TILES = 8

MXU_DTYPE = jnp.bfloat16
F32 = jnp.float32
VMEM_LIMIT = 56 * 1024 * 1024


def _cparams(sem):
    return pltpu.CompilerParams(dimension_semantics=sem, vmem_limit_bytes=VMEM_LIMIT)


def _mod_kernel(c_ref, w_ref, b_ref, o_ref):
    c = c_ref[...]
    ca = c * jax.nn.sigmoid(c)
    o_ref[...] = jnp.dot(ca.astype(MXU_DTYPE), w_ref[...].astype(MXU_DTYPE),
                         preferred_element_type=F32) + b_ref[...]


def _mod(c8, w_mod, b_mod, tn=512):
    rows, d = c8.shape
    n = w_mod.shape[1]
    return pl.pallas_call(
        _mod_kernel,
        grid=(n // tn,),
        in_specs=[pl.BlockSpec((rows, d), lambda j: (0, 0)),
                  pl.BlockSpec((d, tn), lambda j: (0, j)),
                  pl.BlockSpec((1, tn), lambda j: (0, j))],
        out_specs=pl.BlockSpec((rows, tn), lambda j: (0, j)),
        out_shape=jax.ShapeDtypeStruct((rows, n), F32),
        compiler_params=_cparams(("arbitrary",)),
        name="mod",
    )(c8, w_mod, b_mod)


def _rms(x, g):
    return x * jax.lax.rsqrt(jnp.mean(x * x, axis=-1, keepdims=True) + EPS) * g


def _norm_mod_kernel(x_ref, g_ref, sh_ref, sc_ref, o_ref):
    y = _rms(x_ref[0], g_ref[...])
    o_ref[0] = (y * (1.0 + sc_ref[0]) + sh_ref[0]).astype(o_ref.dtype)


def _norm_mod(x, g, sh, sc, tm=256):
    b, t, d = x.shape
    row = pl.BlockSpec((1, 1, d), lambda i, j: (i, 0, 0))
    return pl.pallas_call(
        _norm_mod_kernel,
        grid=(b, t // tm),
        in_specs=[pl.BlockSpec((1, tm, d), lambda i, j: (i, j, 0)),
                  pl.BlockSpec((1, d), lambda i, j: (0, 0)), row, row],
        out_specs=pl.BlockSpec((1, tm, d), lambda i, j: (i, j, 0)),
        out_shape=jax.ShapeDtypeStruct((b, t, d), MXU_DTYPE),
        compiler_params=_cparams(("arbitrary", "arbitrary")),
        name="norm_mod",
    )(x, g, sh, sc)


def _resid_norm_mod_kernel(x_ref, y_ref, g1_ref, gt_ref, g2_ref, sh_ref, sc_ref, x1_ref, h_ref):
    x1 = x_ref[0] + gt_ref[0] * _rms(y_ref[0], g1_ref[...])
    x1_ref[0] = x1
    h_ref[0] = (_rms(x1, g2_ref[...]) * (1.0 + sc_ref[0]) + sh_ref[0]).astype(h_ref.dtype)


def _resid_norm_mod(x, y, g1, gt, g2, sh, sc, tm=256):
    b, t, d = x.shape
    row = pl.BlockSpec((1, 1, d), lambda i, j: (i, 0, 0))
    gspec = pl.BlockSpec((1, d), lambda i, j: (0, 0))
    blk = pl.BlockSpec((1, tm, d), lambda i, j: (i, j, 0))
    return pl.pallas_call(
        _resid_norm_mod_kernel,
        grid=(b, t // tm),
        in_specs=[blk, blk, gspec, row, gspec, row, row],
        out_specs=[blk, blk],
        out_shape=[jax.ShapeDtypeStruct((b, t, d), F32), jax.ShapeDtypeStruct((b, t, d), MXU_DTYPE)],
        compiler_params=_cparams(("arbitrary", "arbitrary")),
        name="resid_norm_mod",
    )(x, y, g1, gt, g2, sh, sc)


def _resid_norm_kernel(x_ref, y_ref, g_ref, gt_ref, o_ref):
    o_ref[0] = x_ref[0] + gt_ref[0] * _rms(y_ref[0], g_ref[...])


def _resid_norm(x, y, g, gt, tm=256):
    b, t, d = x.shape
    blk = pl.BlockSpec((1, tm, d), lambda i, j: (i, j, 0))
    return pl.pallas_call(
        _resid_norm_kernel,
        grid=(b, t // tm),
        in_specs=[blk, blk, pl.BlockSpec((1, d), lambda i, j: (0, 0)),
                  pl.BlockSpec((1, 1, d), lambda i, j: (i, 0, 0))],
        out_specs=blk,
        out_shape=jax.ShapeDtypeStruct((b, t, d), F32),
        compiler_params=_cparams(("arbitrary", "arbitrary")),
        name="resid_norm",
    )(x, y, g, gt)


def _mm_kernel(*refs, nk, scaled):
    if scaled:
        x_ref, w_ref, s_ref, o_ref = refs[:4]
        rest = refs[4:]
    else:
        x_ref, w_ref, o_ref = refs[:3]
        s_ref = None
        rest = refs[3:]

    def finish(acc):
        if scaled:
            acc = acc * s_ref[...]
        o_ref[...] = acc.astype(o_ref.dtype)

    part = jnp.dot(x_ref[...], w_ref[...], preferred_element_type=F32)
    if nk == 1:
        finish(part)
        return
    acc_ref, = rest
    k = pl.program_id(2)

    @pl.when(k == 0)
    def _():
        acc_ref[...] = jnp.zeros(acc_ref.shape, F32)

    acc_ref[...] += part

    @pl.when(k == nk - 1)
    def _():
        finish(acc_ref[...])


def _matmul(x, w, *, tm, tn, tk, out_dtype, col_scale=None, name="matmul"):
    m, kdim = x.shape
    n = w.shape[1]
    nk = kdim // tk
    in_specs = [pl.BlockSpec((tm, tk), lambda i, j, k: (i, k)),
                pl.BlockSpec((tk, tn), lambda i, j, k: (k, j))]
    args = [x, w]
    if col_scale is not None:
        in_specs.append(pl.BlockSpec((1, tn), lambda i, j, k: (0, j)))
        args.append(col_scale)
    return pl.pallas_call(
        functools.partial(_mm_kernel, nk=nk, scaled=col_scale is not None),
        grid=(m // tm, n // tn, nk),
        in_specs=in_specs,
        out_specs=pl.BlockSpec((tm, tn), lambda i, j, k: (i, j)),
        out_shape=jax.ShapeDtypeStruct((m, n), out_dtype),
        scratch_shapes=[] if nk == 1 else [pltpu.VMEM((tm, tn), F32)],
        compiler_params=_cparams(("arbitrary", "arbitrary", "arbitrary")),
        name=name,
    )(*args)


def _compress_kernel(a_ref, w1_ref, pos_ref, b1_ref, w2_ref, b2_ref, o_ref, *, n_cmp):
    a = a_ref[0, 0, 0]
    w1 = w1_ref[0]
    half = a.shape[1]
    top = jnp.dot(a, w1[:half], preferred_element_type=F32)
    bot = jnp.dot(a, w1[half:], preferred_element_type=F32)
    rows = a.shape[0]
    pre = top + pltpu.roll(bot, rows - 1, 0)
    pos_term = jnp.dot(pos_ref[0].astype(MXU_DTYPE), w1, preferred_element_type=F32)[0:1]
    hid = jax.nn.gelu(pre + pos_term + b1_ref[0])
    out = jnp.dot(hid.astype(MXU_DTYPE), w2_ref[0], preferred_element_type=F32) + b2_ref[0]
    ridx = jax.lax.broadcasted_iota(jnp.int32, out.shape, 0)
    o_ref[0, 0, 0] = jnp.where(ridx < n_cmp, out, 0.0)


def _compress(a, w1, pos8, b1, w2, b2, n_cmp):
    b, two, g, rows, width = a.shape
    hid = w1.shape[2]
    dk = w2.shape[2]
    return pl.pallas_call(
        functools.partial(_compress_kernel, n_cmp=n_cmp),
        grid=(two, b, g),
        in_specs=[pl.BlockSpec((1, 1, 1, rows, width), lambda s, i, j: (i, s, j, 0, 0)),
                  pl.BlockSpec((1, 2 * width, hid), lambda s, i, j: (s, 0, 0)),
                  pl.BlockSpec((1, 8, 2 * width), lambda s, i, j: (s, 0, 0)),
                  pl.BlockSpec((1, 1, hid), lambda s, i, j: (s, 0, 0)),
                  pl.BlockSpec((1, hid, dk), lambda s, i, j: (s, 0, 0)),
                  pl.BlockSpec((1, 1, dk), lambda s, i, j: (s, 0, 0))],
        out_specs=pl.BlockSpec((1, 1, 1, rows, dk), lambda s, i, j: (i, s, j, 0, 0)),
        out_shape=jax.ShapeDtypeStruct((b, two, g, rows, dk), F32),
        compiler_params=_cparams(("arbitrary", "arbitrary", "arbitrary")),
        name="compress",
    )(a, w1, pos8, b1, w2, b2)


def _rel_bucket_np(dist):
    n = np.maximum(dist, 0)
    max_exact = REL_BUCKETS // 2
    nf = np.maximum(n, 1).astype(np.float32)
    large = max_exact + (np.log(nf / max_exact) / math.log(REL_MAX_DIST / max_exact)
                         * (REL_BUCKETS - max_exact)).astype(np.int32)
    large = np.minimum(large, REL_BUCKETS - 1)
    return np.where(n < max_exact, n, large).astype(np.int32)


def _bias_index_tiles():
    w = np.arange(WINDOW + Q_BLOCK)[:, None]
    i = np.arange(Q_BLOCK)[None, :]
    d_w = i - w + WINDOW
    win = np.where((d_w >= 0) & (d_w < WINDOW), _rel_bucket_np(d_w), -1)
    m = np.arange(CMP_NEAR)[:, None]
    d_c = i - CMP_STRIDE * m + (CMP_STRIDE * CMP_FRONT - CMP_BLOCK + 1)
    near = np.where(d_c >= 0, _rel_bucket_np(d_c), -1)
    return win.astype(np.int32), near.astype(np.int32)


def _bias_kernel(tab_ref, idw_ref, idc_ref, ow_ref, oc_ref, o31_ref):
    h = pl.program_id(0)

    def lut(idx):
        out = jnp.full(idx.shape, NEG, F32)
        for b in range(REL_BUCKETS):
            out = jnp.where(idx == b, tab_ref[b, h] * LOG2E, out)
        return out

    ow_ref[0] = lut(idw_ref[...])
    oc_ref[0] = lut(idc_ref[...])
    o31_ref[0] = jnp.full(o31_ref.shape[1:], tab_ref[REL_BUCKETS - 1, h] * LOG2E, F32)


def _bias_tiles(rel_table):
    win, near = _bias_index_tiles()
    g, r = NSA_KV_GROUPS, HEADS_PER_GROUP
    full = lambda a: pl.BlockSpec(a.shape, lambda h: (0, 0))
    head = lambda rows: pl.BlockSpec((1, rows, Q_BLOCK), lambda h: (h // r, 0, h % r))
    return pl.pallas_call(
        _bias_kernel,
        grid=(NSA_HEADS,),
        in_specs=[pl.BlockSpec(memory_space=pltpu.SMEM), full(win), full(near)],
        out_specs=[head(win.shape[0]), head(CMP_NEAR), head(8)],
        out_shape=[jax.ShapeDtypeStruct((g, win.shape[0], QL), F32),
                   jax.ShapeDtypeStruct((g, CMP_NEAR, QL), F32), jax.ShapeDtypeStruct((g, 8, QL), F32)],
        compiler_params=_cparams(("arbitrary",)),
        name="bias_tiles",
    )(rel_table, jnp.asarray(win), jnp.asarray(near))


def _bias_variants(bwin_full):
    g = bwin_full.shape[0]
    n_win = WINDOW // Q_BLOCK
    tiles = bwin_full.reshape(g, n_win + 1, Q_BLOCK, QL)
    masked = jnp.full((g, Q_BLOCK, QL), NEG, F32)

    def variants(n_tiles):
        out = []
        for v in range(n_tiles):
            chunk = [tiles[:, n_win - v + u] if v - u >= 0 else masked for u in range(n_tiles)]
            out.append(jnp.concatenate(chunk, axis=1))
        return jnp.stack(out, axis=1)

    return variants(n_win + 1), variants(2)


def _selection_map_t(n_cmp, n_slc, ncp):
    ratio = SLC_BLOCK // CMP_STRIDE
    span = CMP_BLOCK // CMP_STRIDE
    d = np.arange(n_cmp)[:, None] - ratio * np.arange(n_slc)[None, :]
    a = d[..., None] + np.arange(span)
    sel = np.sum((a >= 0) & (a < ratio), axis=-1).astype(np.float32)
    out = np.zeros((n_slc, ncp), np.float32)
    out[:, CMP_FRONT:CMP_FRONT + n_cmp] = sel.T
    return out


def _nsa_kernel(q_ref, kc_ref, vc_ref, selt_ref, ks_ref, vs_ref, kw_ref, vw_ref, g_ref,
                bnear_ref, bwin_ref, bc_ref, c31_ref, o_ref,
                p_scr, selb_scr, selbf_scr, m_scr, l_scr, acc_scr, *, top_k):
    qb = pl.program_id(2)
    qn = q_ref[0]
    q = jnp.concatenate([qn[:, r * HEAD_DIM:(r + 1) * HEAD_DIM].astype(F32).T for r in range(HEADS_PER_GROUP)],
                        axis=1).astype(MXU_DTYPE)
    ncp = kc_ref.shape[3]
    n_slc = selt_ref.shape[0]
    c31 = c31_ref[0][0:1, :]
    lane_i = jax.lax.broadcasted_iota(jnp.int32, (1, QL), 1) & (Q_BLOCK - 1)
    t0 = qb * Q_BLOCK

    def vt_dot(v_rows, p):
        return jax.lax.dot_general(v_rows, p.astype(MXU_DTYPE), (((0,), (0,)), ((), ())),
                                   preferred_element_type=F32)

    def chunk_softmax(k_rows, v_rows, bias):
        s = jnp.dot(k_rows, q, preferred_element_type=F32) + bias
        m_c = jnp.max(s, axis=0, keepdims=True)
        pe = jnp.exp2(s - m_c)
        return m_c, jnp.sum(pe, axis=0, keepdims=True), vt_dot(v_rows, pe)

    def keys(ref, tile, n_tiles):
        return ref[0, pl.ds(pl.multiple_of(tile * Q_BLOCK, Q_BLOCK), n_tiles * Q_BLOCK), :]

    def block_mask(scr, tile, n_tiles):
        blocks_per_tile = Q_BLOCK // SLC_BLOCK
        rows = [jnp.broadcast_to(scr[pl.ds(blocks_per_tile * tile + u, 1), :], (SLC_BLOCK, QL))
                for u in range(blocks_per_tile * n_tiles)]
        return jnp.concatenate(rows, axis=0)

    n_win = WINDOW // Q_BLOCK
    win_tile = jnp.maximum(qb - n_win, 0)
    _, l_w, acc_w = chunk_softmax(keys(kw_ref, win_tile, n_win + 1), keys(vw_ref, win_tile, n_win + 1),
                                  bwin_ref[0, 0])
    o_w = acc_w * (1.0 / l_w)

    s_all = jnp.dot(kc_ref[0, 0, 0].astype(MXU_DTYPE), q, preferred_element_type=F32)
    prow = jax.lax.broadcasted_iota(jnp.int32, (ncp, QL), 0)
    near0 = pl.multiple_of(qb * (Q_BLOCK // CMP_STRIDE), 8)
    s_far = jnp.where(prow < near0, jnp.where(prow >= CMP_FRONT, s_all + c31, NEG), NEG)
    kn = kc_ref[0, 0, 0, pl.ds(near0, CMP_NEAR), :]
    mrow = jax.lax.broadcasted_iota(jnp.int32, (CMP_NEAR, QL), 0)
    s_near = jnp.dot(kn.astype(MXU_DTYPE), q, preferred_element_type=F32) + bc_ref[0]
    s_near = jnp.where(mrow >= CMP_FRONT - near0, s_near, NEG)
    mx = jnp.maximum(jnp.max(s_far, axis=0, keepdims=True), jnp.max(s_near, axis=0, keepdims=True))
    e_far = jnp.exp2(s_far - mx)
    e_near = jnp.exp2(s_near - mx)
    den = jnp.sum(e_far, axis=0, keepdims=True) + jnp.sum(e_near, axis=0, keepdims=True)
    inv = jnp.where(t0 + lane_i >= CMP_BLOCK - 1, 1.0 / den, 0.0)
    p_scr[...] = e_far * inv
    p_scr[pl.ds(near0, CMP_NEAR), :] = e_near * inv
    p = p_scr[...]
    o_c = vt_dot(vc_ref[0, 0, 0].astype(MXU_DTYPE), p)

    psum = p[:, 0:Q_BLOCK]
    for r in range(1, HEADS_PER_GROUP):
        psum = psum + p[:, r * Q_BLOCK:(r + 1) * Q_BLOCK]
    p_hi = psum.astype(MXU_DTYPE)
    p_lo = (psum - p_hi.astype(F32)).astype(MXU_DTYPE)
    selt = selt_ref[...]
    imp = (jnp.dot(selt, p_hi, preferred_element_type=F32)
           + jnp.dot(selt, p_lo, preferred_element_type=F32))
    jidx = jax.lax.broadcasted_iota(jnp.int32, (n_slc, Q_BLOCK), 0).astype(F32)
    iidx = jax.lax.broadcasted_iota(jnp.int32, (n_slc, Q_BLOCK), 1)
    cur = (qb * (Q_BLOCK // SLC_BLOCK) + jax.lax.shift_right_logical(iidx, int(math.log2(SLC_BLOCK)))).astype(F32)
    vis = jidx <= cur
    forced = jnp.where(jidx == 0.0, 1.0, jnp.where(jidx == cur, 1.0, jnp.where(jidx == cur - 1.0, 1.0, 0.0)))
    score = jnp.where(vis, jnp.where(forced > 0.5, FORCE, imp), -1.0)
    rem = score
    chosen = jnp.zeros((n_slc, Q_BLOCK), F32)
    for _ in range(top_k):
        best = jnp.max(rem, axis=0, keepdims=True)
        first = jnp.min(jnp.where(rem == best, jidx, float(n_slc)), axis=0, keepdims=True)
        hit = jidx == first
        chosen = jnp.where(hit, 1.0, chosen)
        rem = jnp.where(hit, -2.0, rem)
    n_far = jnp.maximum(qb - 1, 0)
    far_block = jidx < (n_far * (Q_BLOCK // SLC_BLOCK)).astype(F32)
    selb = jnp.where(vis, jnp.where(chosen > 0.5, 0.0, NEG), NEG)
    selbf = jnp.where(far_block, selb, NEG)
    selb_scr[...] = jnp.concatenate([selb] * HEADS_PER_GROUP, axis=1)
    selbf_scr[...] = jnp.concatenate([selbf] * HEADS_PER_GROUP, axis=1) + c31

    near_tile = jnp.maximum(qb - 1, 0)
    m0, l0, acc0 = chunk_softmax(keys(ks_ref, near_tile, 2), keys(vs_ref, near_tile, 2),
                                 block_mask(selb_scr, near_tile, 2) + bnear_ref[0, 0])
    m_scr[...] = m0
    l_scr[...] = l0
    acc_scr[...] = acc0

    def far_chunk(c, carry):
        m_c, l_c, acc_c = chunk_softmax(keys(ks_ref, c * FAR_TILES, FAR_TILES),
                                        keys(vs_ref, c * FAR_TILES, FAR_TILES),
                                        block_mask(selbf_scr, c * FAR_TILES, FAR_TILES))
        m_old = m_scr[...]
        m_new = jnp.maximum(m_old, m_c)
        a_old = jnp.exp2(m_old - m_new)
        a_c = jnp.exp2(m_c - m_new)
        l_scr[...] = a_old * l_scr[...] + a_c * l_c
        acc_scr[...] = a_old * acc_scr[...] + a_c * acc_c
        m_scr[...] = m_new
        return carry

    jax.lax.fori_loop(0, (n_far + FAR_TILES - 1) // FAR_TILES, far_chunk, 0)
    o_s = acc_scr[...] * (1.0 / l_scr[...])

    gates = jax.nn.sigmoid(g_ref[0, 0, 0])
    o = gates[0:1, :] * o_c + gates[1:2, :] * o_s + gates[2:3, :] * o_w
    for r in range(HEADS_PER_GROUP):
        o_ref[0, :, r * HEAD_DIM:(r + 1) * HEAD_DIM] = o[:, r * Q_BLOCK:(r + 1) * Q_BLOCK].T.astype(o_ref.dtype)


def _nsa(att, cmp_pad, selt, kv_col, gt, bnear, bwin, bc, c31):
    b, seq, _ = att.shape
    g = cmp_pad.shape[2]
    nq = seq // Q_BLOCK
    ncp = cmp_pad.shape[3]
    n_slc = selt.shape[0]
    kv_block = lambda which: pl.BlockSpec((1, seq, HEAD_DIM), lambda i, j, k: (i, 0, kv_col + which * g + j))
    cmp_block = lambda which: pl.BlockSpec((1, 1, 1, ncp, HEAD_DIM), lambda i, j, k: (i, which, j, 0, 0))
    top_k = min(SLC_TOPK, n_slc)
    assert nq % FAR_TILES == 0 and nq > WINDOW // Q_BLOCK
    per_g = lambda rows: pl.BlockSpec((1, rows, QL), lambda i, j, k: (j, 0, 0))
    variant = lambda a: pl.BlockSpec((1, 1) + a.shape[2:], lambda i, j, k: (j, jnp.minimum(k, a.shape[1] - 1), 0, 0))
    return pl.pallas_call(
        functools.partial(_nsa_kernel, top_k=top_k),
        grid=(b, g, nq),
        in_specs=[pl.BlockSpec((1, Q_BLOCK, QL), lambda i, j, k: (i, k, j)),
                  cmp_block(0), cmp_block(1),
                  pl.BlockSpec((n_slc, ncp), lambda i, j, k: (0, 0)),
                  kv_block(0), kv_block(1), kv_block(2), kv_block(3),
                  pl.BlockSpec((1, 1, 1, 8, QL), lambda i, j, k: (i, j, k, 0, 0)),
                  variant(bnear), variant(bwin), per_g(CMP_NEAR), per_g(8)],
        out_specs=pl.BlockSpec((1, Q_BLOCK, QL), lambda i, j, k: (i, k, j)),
        out_shape=jax.ShapeDtypeStruct((b, seq, g * QL), MXU_DTYPE),
        scratch_shapes=[pltpu.VMEM((ncp, QL), F32), pltpu.VMEM((n_slc, QL), F32), pltpu.VMEM((n_slc, QL), F32),
                        pltpu.VMEM((1, QL), F32), pltpu.VMEM((1, QL), F32), pltpu.VMEM((HEAD_DIM, QL), F32)],
        compiler_params=_cparams(("arbitrary", "arbitrary", "arbitrary")),
        name="nsa",
    )(att, cmp_pad, cmp_pad, selt, att, att, att, att, gt, bnear, bwin, bc, c31)


def _sgu_kernel(zu_ref, zv_ref, lg_ref, lb_ref, w_ref, bs_ref, o_ref):
    u = jax.nn.gelu(zu_ref[0].astype(F32))
    v = jax.nn.gelu(zv_ref[0].astype(F32))
    mu = jnp.mean(v, axis=-1, keepdims=True)
    var = jnp.mean(jnp.square(v - mu), axis=-1, keepdims=True)
    vn = ((v - mu) * jax.lax.rsqrt(var + EPS) * lg_ref[...] + lb_ref[...]).astype(MXU_DTYPE)
    tc = w_ref.shape[1]
    causal = (jax.lax.broadcasted_iota(jnp.int32, (tc, tc), 0)
              >= jax.lax.broadcasted_iota(jnp.int32, (tc, tc), 1))
    bs = bs_ref[...]
    for h in range(w_ref.shape[0]):
        sl = slice(h * SGU_GROUP_DIM, (h + 1) * SGU_GROUP_DIM)
        w = jnp.where(causal, w_ref[h], 0.0).astype(MXU_DTYPE)
        mixed = jnp.dot(w, vn[:, sl], preferred_element_type=F32) + bs[:, h:h + 1]
        o_ref[0, :, sl] = (u[:, sl] * mixed).astype(o_ref.dtype)


def _sgu(proj3, ln_g, ln_b, w_s, b_st):
    b, t, _ = proj3.shape
    gs, tc, _ = w_s.shape
    vec = pl.BlockSpec((1, SGU_WIDTH), lambda i, j: (0, 0))
    return pl.pallas_call(
        _sgu_kernel,
        grid=(b, t // tc),
        in_specs=[pl.BlockSpec((1, tc, SGU_WIDTH), lambda i, j: (i, j, 0)),
                  pl.BlockSpec((1, tc, SGU_WIDTH), lambda i, j: (i, j, 1)),
                  vec, vec,
                  pl.BlockSpec((gs, tc, tc), lambda i, j: (0, 0, 0)),
                  pl.BlockSpec((tc, gs), lambda i, j: (0, 0))],
        out_specs=pl.BlockSpec((1, tc, SGU_WIDTH), lambda i, j: (i, j, 0)),
        out_shape=jax.ShapeDtypeStruct((b, t, SGU_WIDTH), MXU_DTYPE),
        compiler_params=_cparams(("arbitrary", "arbitrary")),
        name="sgu",
    )(proj3, proj3, ln_g, ln_b, w_s, b_st)


def _merge_kernel(ya_ref, yb_ref, wa_ref, wb_ref, ga_ref, gb_ref, o_ref):
    pa = jnp.dot(ya_ref[...], wa_ref[...], preferred_element_type=F32)
    pb = jnp.dot(yb_ref[...], wb_ref[...], preferred_element_type=F32)
    merged = (jax.nn.sigmoid(ga_ref[...].astype(F32)) * pa + jax.nn.sigmoid(gb_ref[...].astype(F32)) * pb)
    o_ref[...] = merged.astype(o_ref.dtype)


def _merge(ya, yb, wa, wb, proj, ga_col, gb_col, *, tm=1024, tn=512):
    m, ka = ya.shape
    kb = yb.shape[1]
    n = wa.shape[1]
    return pl.pallas_call(
        _merge_kernel,
        grid=(m // tm, n // tn),
        in_specs=[pl.BlockSpec((tm, ka), lambda i, j: (i, 0)),
                  pl.BlockSpec((tm, kb), lambda i, j: (i, 0)),
                  pl.BlockSpec((ka, tn), lambda i, j: (0, j)),
                  pl.BlockSpec((kb, tn), lambda i, j: (0, j)),
                  pl.BlockSpec((tm, tn), lambda i, j: (i, ga_col + j)),
                  pl.BlockSpec((tm, tn), lambda i, j: (i, gb_col + j))],
        out_specs=pl.BlockSpec((tm, tn), lambda i, j: (i, j)),
        out_shape=jax.ShapeDtypeStruct((m, n), MXU_DTYPE),
        compiler_params=_cparams(("arbitrary", "arbitrary")),
        name="merge",
    )(ya, yb, wa, wb, proj, proj)


def _ffn_up_kernel(h_ref, halo_ref, wg_ref, wu_ref, cwg_ref, cwu_ref, cbg_ref, cbu_ref, o_ref, *, seq, sub):
    i = pl.program_id(0)
    tm = h_ref.shape[0]
    wg = wg_ref[...]
    wu = wu_ref[...]
    row = jax.lax.broadcasted_iota(jnp.int32, (sub, wg.shape[1]), 0)

    def conv(a, before, cw_ref, cb_ref):
        a1 = jnp.where(row == 0, before[7:8, :], pltpu.roll(a, 1, 0))
        a2 = jnp.where(row == 0, before[6:7, :], jnp.where(row == 1, before[7:8, :], pltpu.roll(a, 2, 0)))
        return cb_ref[0:1, :] + a2 * cw_ref[0:1, :] + a1 * cw_ref[1:2, :] + a * cw_ref[2:3, :]

    keep = jnp.where((i * tm) % seq == 0, 0.0, 1.0)
    before_g = jnp.dot(halo_ref[...], wg, preferred_element_type=F32) * keep
    before_u = jnp.dot(halo_ref[...], wu, preferred_element_type=F32) * keep
    for s in range(tm // sub):
        hs = h_ref[s * sub:(s + 1) * sub, :]
        ag = jnp.dot(hs, wg, preferred_element_type=F32)
        au = jnp.dot(hs, wu, preferred_element_type=F32)
        gate = conv(ag, before_g, cwg_ref, cbg_ref)
        up = conv(au, before_u, cwu_ref, cbu_ref)
        o_ref[s * sub:(s + 1) * sub, :] = (gate * jax.nn.sigmoid(gate) * up).astype(o_ref.dtype)
        before_g = ag[sub - 8:, :]
        before_u = au[sub - 8:, :]


def _ffn_up(h, w, cw8, cb8, *, seq, tm=2048, sub=512):
    m, d = h.shape
    f = w.shape[1] // 2
    tn = FFN_HALF_TILE
    nj = f // tn
    halo_blocks = tm // 8
    gate_cols = lambda rows: pl.BlockSpec((rows, tn), lambda i, j: (0, j))
    up_cols = lambda rows: pl.BlockSpec((rows, tn), lambda i, j: (0, nj + j))
    return pl.pallas_call(
        functools.partial(_ffn_up_kernel, seq=seq, sub=sub),
        grid=(m // tm, nj),
        in_specs=[pl.BlockSpec((tm, d), lambda i, j: (i, 0)),
                  pl.BlockSpec((8, d), lambda i, j: (jnp.maximum(i * halo_blocks - 1, 0), 0)),
                  gate_cols(d), up_cols(d), gate_cols(8), up_cols(8), gate_cols(8), up_cols(8)],
        out_specs=pl.BlockSpec((tm, tn), lambda i, j: (i, j)),
        out_shape=jax.ShapeDtypeStruct((m, f), MXU_DTYPE),
        compiler_params=_cparams(("arbitrary", "arbitrary")),
        name="ffn_up",
    )(h, h, w, w, cw8, cw8, cb8, cb8)


def _pad_rows(a, rows):
    return jnp.pad(a, ((0, rows - a.shape[0]), (0, 0)))


def _mixer_branches(h, w_in, cmp_pos, cmp_w1, cmp_b1, cmp_w2, cmp_b2, rel_table,
                    sgu_ln_g, sgu_ln_b, sgu_w, sgu_b):
    b, t, d = h.shape
    g, r, dk = NSA_KV_GROUPS, HEADS_PER_GROUP, HEAD_DIM
    m = b * t
    nq = t // Q_BLOCK
    n_cmp = (t - CMP_BLOCK) // CMP_STRIDE + 1
    n_slc = t // SLC_BLOCK
    rows16 = t // CMP_STRIDE
    ncp = rows16 + LANES

    o_gn = NSA_WIDTH + 6 * KV_WIDTH
    o_z = o_gn + NSA_HEADS * 3
    w_att = w_in[:, :o_gn].astype(MXU_DTYPE)
    w_gn = jnp.pad(w_in[:, o_gn:o_z], ((0, 0), (0, LANES - NSA_HEADS * 3))).astype(MXU_DTYPE)
    w_rest = w_in[:, o_z:].astype(MXU_DTYPE)
    c_ga, c_gb = 2 * SGU_WIDTH, 2 * SGU_WIDTH + d
    c_kv = [NSA_WIDTH + i * KV_WIDTH for i in range(6)]
    col_scale = jnp.ones((1, o_gn), F32).at[:, :NSA_WIDTH].set(HEAD_DIM ** -0.5 * LOG2E)

    h2d = h.reshape(m, d)
    proj = _matmul(h2d, w_rest, tm=1024, tn=1024, tk=d, out_dtype=MXU_DTYPE, name="proj_rest")
    patt = _matmul(h2d, w_att, tm=1024, tn=1024, tk=d, out_dtype=MXU_DTYPE, col_scale=col_scale, name="proj_att")
    gn = _matmul(h2d, w_gn, tm=1024, tn=LANES, tk=d, out_dtype=F32, name="proj_gates")
    proj3 = proj.reshape(b, t, proj.shape[1])
    patt3 = patt.reshape(b, t, o_gn)

    kv_c = patt3[:, :, c_kv[0]:c_kv[2]].reshape(b, rows16, CMP_STRIDE, 2, g, dk)
    a_cmp = kv_c.transpose(0, 3, 4, 1, 2, 5).reshape(b, 2, g, rows16, CMP_STRIDE * dk)
    pos8 = jnp.broadcast_to(cmp_pos.reshape(2, 1, CMP_BLOCK * dk), (2, 8, CMP_BLOCK * dk))
    cmp_out = _compress(a_cmp, cmp_w1.astype(MXU_DTYPE), pos8, cmp_b1.reshape(2, 1, CMP_HIDDEN),
                        cmp_w2.astype(MXU_DTYPE), cmp_b2.reshape(2, 1, dk), n_cmp)
    cmp_pad = jnp.pad(cmp_out, ((0, 0), (0, 0), (0, 0), (CMP_FRONT, ncp - CMP_FRONT - rows16), (0, 0)))

    gt = gn[:, :NSA_HEADS * 3].reshape(b, nq, Q_BLOCK, g, r, 3).transpose(0, 3, 1, 5, 4, 2).reshape(b, g, nq, 3, QL)
    gt = jnp.pad(gt, ((0, 0), (0, 0), (0, 0), (0, 5), (0, 0)))
    bwin_full, bc, c31 = _bias_tiles(rel_table)
    bwin, bnear = _bias_variants(bwin_full)
    selt = jnp.asarray(_selection_map_t(n_cmp, n_slc, ncp), MXU_DTYPE)
    y_a = _nsa(patt3, cmp_pad, selt, c_kv[2] // dk, gt, bnear, bwin, bc, c31)

    y_b = _sgu(proj3, sgu_ln_g.reshape(1, SGU_WIDTH), sgu_ln_b.reshape(1, SGU_WIDTH), sgu_w, sgu_b.T)
    return proj, y_a, y_b, c_ga, c_gb


def _token_mixer(h, w_in, cmp_pos, cmp_w1, cmp_b1, cmp_w2, cmp_b2, rel_table,
                 sgu_ln_g, sgu_ln_b, sgu_w, sgu_b, w_proj_nsa, w_proj_sgu):
    proj, y_a, y_b, c_ga, c_gb = _mixer_branches(h, w_in, cmp_pos, cmp_w1, cmp_b1, cmp_w2, cmp_b2, rel_table,
                                                 sgu_ln_g, sgu_ln_b, sgu_w, sgu_b)
    m = proj.shape[0]
    tn = 512
    return _merge(y_a.reshape(m, NSA_WIDTH), y_b.reshape(m, SGU_WIDTH),
                  w_proj_nsa.astype(MXU_DTYPE), w_proj_sgu.astype(MXU_DTYPE),
                  proj, c_ga // tn, c_gb // tn, tm=1024, tn=tn)


def kernel(x, c, w_mod, b_mod, g_norms, w_in, cmp_pos, cmp_w1, cmp_b1, cmp_w2, cmp_b2, rel_table, sgu_ln_g, sgu_ln_b, sgu_w, sgu_b, w_proj_nsa, w_proj_sgu, w_out, w_ffn_up, ffn_conv_w, ffn_conv_b, w_ffn_down):
    b, t, d = x.shape
    m = b * t
    depth = w_mod.shape[0]
    c8 = jnp.pad(c, ((0, 8 - b), (0, 0)))
    for l in range(depth):
        mod = _mod(c8, w_mod[l], b_mod[l].reshape(1, -1))[:b]
        sh1, sc1, gt1, sh2, sc2, gt2 = [v.reshape(b, 1, d) for v in jnp.split(mod, 6, axis=-1)]
        gn = g_norms[l]

        h = _norm_mod(x, gn[0:1], sh1, sc1)
        merged = _token_mixer(h, w_in[l], cmp_pos[l], cmp_w1[l], cmp_b1[l], cmp_w2[l], cmp_b2[l], rel_table,
                              sgu_ln_g[l], sgu_ln_b[l], sgu_w[l], sgu_b[l], w_proj_nsa[l], w_proj_sgu[l])
        y = _matmul(merged, w_out[l].astype(MXU_DTYPE), tm=1024, tn=1024, tk=d, out_dtype=F32, name="out_proj")
        x1, h2 = _resid_norm_mod(x, y.reshape(b, t, d), gn[1:2], gt1, gn[2:3], sh2, sc2)

        cw8 = _pad_rows(ffn_conv_w[l], 8)
        cb8 = _pad_rows(ffn_conv_b[l].reshape(1, -1), 8)
        act = _ffn_up(h2.reshape(m, d), w_ffn_up[l].astype(MXU_DTYPE), cw8, cb8, seq=t)
        d_ff = act.shape[1]
        y2 = _matmul(act, w_ffn_down[l].astype(MXU_DTYPE), tm=512, tn=512, tk=d_ff, out_dtype=F32, name="ffn_down")
        x = _resid_norm(x1, y2.reshape(b, t, d), gn[3:4], gt2)
    return x
```

```python
import functools
import math

import jax
import jax.numpy as jnp
import numpy as np
from jax.experimental import pallas as pl
from jax.experimental.pallas import tpu as pltpu

NSA_HEADS = 16
NSA_KV_GROUPS = 4
HEADS_PER_GROUP = NSA_HEADS // NSA_KV_GROUPS
HEAD_DIM = 128
CMP_BLOCK = 32
CMP_STRIDE = 16
CMP_HIDDEN = 256
SLC_BLOCK = 64
SLC_TOPK = 16
WINDOW = 512
Q_BLOCK = 256
SGU_GROUPS = 16
SGU_GROUP_DIM = 128
SGU_CHUNK = 128
SGU_WIDTH = SGU_GROUPS * SGU_GROUP_DIM
REL_BUCKETS = 32
REL_MAX_DIST = 128
CONV_WIDTH = 3
EPS = 1e-6
NEG = -1e30
FORCE = 1e6
LOG2E = math.log2(math.e)

NSA_WIDTH = NSA_HEADS * HEAD_DIM
KV_WIDTH = NSA_KV_GROUPS * HEAD_DIM
LANES = 128
QL = HEADS_PER_GROUP * Q_BLOCK
GROUP_WIDTH = HEADS_PER_GROUP * HEAD_DIM
CMP_FRONT = 16
CMP_NEAR = 32
FFN_HALF_TILE = 256
FAR_TILES = 4

MXU_DTYPE = jnp.bfloat16
F32 = jnp.float32
VMEM_LIMIT = 56 * 1024 * 1024


def _cparams(sem):
    return pltpu.CompilerParams(dimension_semantics=sem, vmem_limit_bytes=VMEM_LIMIT)


def _mod_kernel(c_ref, w_ref, b_ref, o_ref):
    c = c_ref[...]
    ca = c * jax.nn.sigmoid(c)
    o_ref[...] = jnp.dot(ca.astype(MXU_DTYPE), w_ref[...].astype(MXU_DTYPE),
                         preferred_element_type=F32) + b_ref[...]


def _mod(c8, w_mod, b_mod, tn=512):
    rows, d = c8.shape
    n = w_mod.shape[1]
    return pl.pallas_call(
        _mod_kernel,
        grid=(n // tn,),
        in_specs=[pl.BlockSpec((rows, d), lambda j: (0, 0)),
                  pl.BlockSpec((d, tn), lambda j: (0, j)),
                  pl.BlockSpec((1, tn), lambda j: (0, j))],
        out_specs=pl.BlockSpec((rows, tn), lambda j: (0, j)),
        out_shape=jax.ShapeDtypeStruct((rows, n), F32),
        compiler_params=_cparams(("arbitrary",)),
        name="mod",
    )(c8, w_mod, b_mod)


def _rms(x, g):
    return x * jax.lax.rsqrt(jnp.mean(x * x, axis=-1, keepdims=True) + EPS) * g


def _norm_mod_kernel(x_ref, g_ref, sh_ref, sc_ref, o_ref):
    y = _rms(x_ref[0], g_ref[...])
    o_ref[0] = (y * (1.0 + sc_ref[0]) + sh_ref[0]).astype(o_ref.dtype)


def _norm_mod(x, g, sh, sc, tm=256):
    b, t, d = x.shape
    row = pl.BlockSpec((1, 1, d), lambda i, j: (i, 0, 0))
    return pl.pallas_call(
        _norm_mod_kernel,
        grid=(b, t // tm),
        in_specs=[pl.BlockSpec((1, tm, d), lambda i, j: (i, j, 0)),
                  pl.BlockSpec((1, d), lambda i, j: (0, 0)), row, row],
        out_specs=pl.BlockSpec((1, tm, d), lambda i, j: (i, j, 0)),
        out_shape=jax.ShapeDtypeStruct((b, t, d), MXU_DTYPE),
        compiler_params=_cparams(("arbitrary", "arbitrary")),
        name="norm_mod",
    )(x, g, sh, sc)


def _resid_norm_mod_kernel(x_ref, y_ref, g1_ref, gt_ref, g2_ref, sh_ref, sc_ref, x1_ref, h_ref):
    x1 = x_ref[0] + gt_ref[0] * _rms(y_ref[0], g1_ref[...])
    x1_ref[0] = x1
    h_ref[0] = (_rms(x1, g2_ref[...]) * (1.0 + sc_ref[0]) + sh_ref[0]).astype(h_ref.dtype)


def _resid_norm_mod(x, y, g1, gt, g2, sh, sc, tm=256):
    b, t, d = x.shape
    row = pl.BlockSpec((1, 1, d), lambda i, j: (i, 0, 0))
    gspec = pl.BlockSpec((1, d), lambda i, j: (0, 0))
    blk = pl.BlockSpec((1, tm, d), lambda i, j: (i, j, 0))
    return pl.pallas_call(
        _resid_norm_mod_kernel,
        grid=(b, t // tm),
        in_specs=[blk, blk, gspec, row, gspec, row, row],
        out_specs=[blk, blk],
        out_shape=[jax.ShapeDtypeStruct((b, t, d), F32), jax.ShapeDtypeStruct((b, t, d), MXU_DTYPE)],
        compiler_params=_cparams(("arbitrary", "arbitrary")),
        name="resid_norm_mod",
    )(x, y, g1, gt, g2, sh, sc)


def _resid_norm_kernel(x_ref, y_ref, g_ref, gt_ref, o_ref):
    o_ref[0] = x_ref[0] + gt_ref[0] * _rms(y_ref[0], g_ref[...])


def _resid_norm(x, y, g, gt, tm=256):
    b, t, d = x.shape
    blk = pl.BlockSpec((1, tm, d), lambda i, j: (i, j, 0))
    return pl.pallas_call(
        _resid_norm_kernel,
        grid=(b, t // tm),
        in_specs=[blk, blk, pl.BlockSpec((1, d), lambda i, j: (0, 0)),
                  pl.BlockSpec((1, 1, d), lambda i, j: (i, 0, 0))],
        out_specs=blk,
        out_shape=jax.ShapeDtypeStruct((b, t, d), F32),
        compiler_params=_cparams(("arbitrary", "arbitrary")),
        name="resid_norm",
    )(x, y, g, gt)


def _mm_kernel(*refs, nk, scaled):
    if scaled:
        x_ref, w_ref, s_ref, o_ref = refs[:4]
        rest = refs[4:]
    else:
        x_ref, w_ref, o_ref = refs[:3]
        s_ref = None
        rest = refs[3:]

    def finish(acc):
        if scaled:
            acc = acc * s_ref[...]
        o_ref[...] = acc.astype(o_ref.dtype)

    part = jnp.dot(x_ref[...], w_ref[...], preferred_element_type=F32)
    if nk == 1:
        finish(part)
        return
    acc_ref, = rest
    k = pl.program_id(2)

    @pl.when(k == 0)
    def _():
        acc_ref[...] = jnp.zeros(acc_ref.shape, F32)

    acc_ref[...] += part

    @pl.when(k == nk - 1)
    def _():
        finish(acc_ref[...])


def _matmul(x, w, *, tm, tn, tk, out_dtype, col_scale=None, name="matmul"):
    m, kdim = x.shape
    n = w.shape[1]
    nk = kdim // tk
    in_specs = [pl.BlockSpec((tm, tk), lambda i, j, k: (i, k)),
                pl.BlockSpec((tk, tn), lambda i, j, k: (k, j))]
    args = [x, w]
    if col_scale is not None:
        in_specs.append(pl.BlockSpec((1, tn), lambda i, j, k: (0, j)))
        args.append(col_scale)
    return pl.pallas_call(
        functools.partial(_mm_kernel, nk=nk, scaled=col_scale is not None),
        grid=(m // tm, n // tn, nk),
        in_specs=in_specs,
        out_specs=pl.BlockSpec((tm, tn), lambda i, j, k: (i, j)),
        out_shape=jax.ShapeDtypeStruct((m, n), out_dtype),
        scratch_shapes=[] if nk == 1 else [pltpu.VMEM((tm, tn), F32)],
        compiler_params=_cparams(("arbitrary", "arbitrary", "arbitrary")),
        name=name,
    )(*args)


def _compress_kernel(a_ref, w1_ref, pos_ref, b1_ref, w2_ref, b2_ref, o_ref, *, n_cmp):
    a = a_ref[0, 0, 0]
    w1 = w1_ref[0]
    half = a.shape[1]
    top = jnp.dot(a, w1[:half], preferred_element_type=F32)
    bot = jnp.dot(a, w1[half:], preferred_element_type=F32)
    rows = a.shape[0]
    pre = top + pltpu.roll(bot, rows - 1, 0)
    pos_term = jnp.dot(pos_ref[0].astype(MXU_DTYPE), w1, preferred_element_type=F32)[0:1]
    hid = jax.nn.gelu(pre + pos_term + b1_ref[0])
    out = jnp.dot(hid.astype(MXU_DTYPE), w2_ref[0], preferred_element_type=F32) + b2_ref[0]
    ridx = jax.lax.broadcasted_iota(jnp.int32, out.shape, 0)
    o_ref[0, 0, 0] = jnp.where(ridx < n_cmp, out, 0.0)


def _compress(a, w1, pos8, b1, w2, b2, n_cmp):
    b, two, g, rows, width = a.shape
    hid = w1.shape[2]
    dk = w2.shape[2]
    return pl.pallas_call(
        functools.partial(_compress_kernel, n_cmp=n_cmp),
        grid=(two, b, g),
        in_specs=[pl.BlockSpec((1, 1, 1, rows, width), lambda s, i, j: (i, s, j, 0, 0)),
                  pl.BlockSpec((1, 2 * width, hid), lambda s, i, j: (s, 0, 0)),
                  pl.BlockSpec((1, 8, 2 * width), lambda s, i, j: (s, 0, 0)),
                  pl.BlockSpec((1, 1, hid), lambda s, i, j: (s, 0, 0)),
                  pl.BlockSpec((1, hid, dk), lambda s, i, j: (s, 0, 0)),
                  pl.BlockSpec((1, 1, dk), lambda s, i, j: (s, 0, 0))],
        out_specs=pl.BlockSpec((1, 1, 1, rows, dk), lambda s, i, j: (i, s, j, 0, 0)),
        out_shape=jax.ShapeDtypeStruct((b, two, g, rows, dk), F32),
        compiler_params=_cparams(("arbitrary", "arbitrary", "arbitrary")),
        name="compress",
    )(a, w1, pos8, b1, w2, b2)


def _rel_bucket_np(dist):
    n = np.maximum(dist, 0)
    max_exact = REL_BUCKETS // 2
    nf = np.maximum(n, 1).astype(np.float32)
    large = max_exact + (np.log(nf / max_exact) / math.log(REL_MAX_DIST / max_exact)
                         * (REL_BUCKETS - max_exact)).astype(np.int32)
    large = np.minimum(large, REL_BUCKETS - 1)
    return np.where(n < max_exact, n, large).astype(np.int32)


def _bias_index_tiles():
    w = np.arange(WINDOW + Q_BLOCK)[:, None]
    i = np.arange(Q_BLOCK)[None, :]
    d_w = i - w + WINDOW
    win = np.where((d_w >= 0) & (d_w < WINDOW), _rel_bucket_np(d_w), -1)
    m = np.arange(CMP_NEAR)[:, None]
    d_c = i - CMP_STRIDE * m + (CMP_STRIDE * CMP_FRONT - CMP_BLOCK + 1)
    near = np.where(d_c >= 0, _rel_bucket_np(d_c), -1)
    return win.astype(np.int32), near.astype(np.int32)


def _bias_kernel(tab_ref, idw_ref, idc_ref, ow_ref, oc_ref, o31_ref):
    h = pl.program_id(0)

    def lut(idx):
        out = jnp.full(idx.shape, NEG, F32)
        for b in range(REL_BUCKETS):
            out = jnp.where(idx == b, tab_ref[b, h] * LOG2E, out)
        return out

    ow_ref[0] = lut(idw_ref[...])
    oc_ref[0] = lut(idc_ref[...])
    o31_ref[0] = jnp.full(o31_ref.shape[1:], tab_ref[REL_BUCKETS - 1, h] * LOG2E, F32)


def _bias_tiles(rel_table):
    win, near = _bias_index_tiles()
    g, r = NSA_KV_GROUPS, HEADS_PER_GROUP
    full = lambda a: pl.BlockSpec(a.shape, lambda h: (0, 0))
    head = lambda rows: pl.BlockSpec((1, rows, Q_BLOCK), lambda h: (h // r, 0, h % r))
    return pl.pallas_call(
        _bias_kernel,
        grid=(NSA_HEADS,),
        in_specs=[pl.BlockSpec(memory_space=pltpu.SMEM), full(win), full(near)],
        out_specs=[head(win.shape[0]), head(CMP_NEAR), head(8)],
        out_shape=[jax.ShapeDtypeStruct((g, win.shape[0], QL), F32),
                   jax.ShapeDtypeStruct((g, CMP_NEAR, QL), F32), jax.ShapeDtypeStruct((g, 8, QL), F32)],
        compiler_params=_cparams(("arbitrary",)),
        name="bias_tiles",
    )(rel_table, jnp.asarray(win), jnp.asarray(near))


def _bias_variants(bwin_full):
    g = bwin_full.shape[0]
    n_win = WINDOW // Q_BLOCK
    tiles = bwin_full.reshape(g, n_win + 1, Q_BLOCK, QL)
    masked = jnp.full((g, Q_BLOCK, QL), NEG, F32)

    def variants(n_tiles):
        out = []
        for v in range(n_tiles):
            chunk = [tiles[:, n_win - v + u] if v - u >= 0 else masked for u in range(n_tiles)]
            out.append(jnp.concatenate(chunk, axis=1))
        return jnp.stack(out, axis=1)

    return variants(n_win + 1), variants(2)


def _selection_map_t(n_cmp, n_slc, ncp):
    ratio = SLC_BLOCK // CMP_STRIDE
    span = CMP_BLOCK // CMP_STRIDE
    d = np.arange(n_cmp)[:, None] - ratio * np.arange(n_slc)[None, :]
    a = d[..., None] + np.arange(span)
    sel = np.sum((a >= 0) & (a < ratio), axis=-1).astype(np.float32)
    out = np.zeros((n_slc, ncp), np.float32)
    out[:, CMP_FRONT:CMP_FRONT + n_cmp] = sel.T
    return out


def _nsa_kernel(q_ref, kc_ref, vc_ref, selt_ref, ks_ref, vs_ref, kw_ref, vw_ref, g_ref,
                bnear_ref, bwin_ref, bc_ref, c31_ref, o_ref,
                p_scr, selb_scr, selbf_scr, m_scr, l_scr, acc_scr, *, top_k):
    qb = pl.program_id(2)
    qn = q_ref[0]
    q = jnp.concatenate([qn[:, r * HEAD_DIM:(r + 1) * HEAD_DIM].astype(F32).T for r in range(HEADS_PER_GROUP)],
                        axis=1).astype(MXU_DTYPE)
    ncp = kc_ref.shape[3]
    n_slc = selt_ref.shape[0]
    c31 = c31_ref[0][0:1, :]
    lane_i = jax.lax.broadcasted_iota(jnp.int32, (1, QL), 1) & (Q_BLOCK - 1)
    t0 = qb * Q_BLOCK

    def vt_dot(v_rows, p):
        return jax.lax.dot_general(v_rows, p.astype(MXU_DTYPE), (((0,), (0,)), ((), ())),
                                   preferred_element_type=F32)

    def chunk_softmax(k_rows, v_rows, bias):
        s = jnp.dot(k_rows, q, preferred_element_type=F32) + bias
        m_c = jnp.max(s, axis=0, keepdims=True)
        pe = jnp.exp2(s - m_c)
        return m_c, jnp.sum(pe, axis=0, keepdims=True), vt_dot(v_rows, pe)

    def keys(ref, tile, n_tiles):
        return ref[0, pl.ds(pl.multiple_of(tile * Q_BLOCK, Q_BLOCK), n_tiles * Q_BLOCK), :]

    def block_mask(scr, tile, n_tiles):
        blocks_per_tile = Q_BLOCK // SLC_BLOCK
        rows = [jnp.broadcast_to(scr[pl.ds(blocks_per_tile * tile + u, 1), :], (SLC_BLOCK, QL))
                for u in range(blocks_per_tile * n_tiles)]
        return jnp.concatenate(rows, axis=0)

    n_win = WINDOW // Q_BLOCK
    win_tile = jnp.maximum(qb - n_win, 0)
    _, l_w, acc_w = chunk_softmax(keys(kw_ref, win_tile, n_win + 1), keys(vw_ref, win_tile, n_win + 1),
                                  bwin_ref[0, 0])
    o_w = acc_w * (1.0 / l_w)

    s_all = jnp.dot(kc_ref[0, 0, 0].astype(MXU_DTYPE), q, preferred_element_type=F32)
    prow = jax.lax.broadcasted_iota(jnp.int32, (ncp, QL), 0)
    near0 = pl.multiple_of(qb * (Q_BLOCK // CMP_STRIDE), 8)
    s_far = jnp.where(prow < near0, jnp.where(prow >= CMP_FRONT, s_all + c31, NEG), NEG)
    kn = kc_ref[0, 0, 0, pl.ds(near0, CMP_NEAR), :]
    mrow = jax.lax.broadcasted_iota(jnp.int32, (CMP_NEAR, QL), 0)
    s_near = jnp.dot(kn.astype(MXU_DTYPE), q, preferred_element_type=F32) + bc_ref[0]
    s_near = jnp.where(mrow >= CMP_FRONT - near0, s_near, NEG)
    mx = jnp.maximum(jnp.max(s_far, axis=0, keepdims=True), jnp.max(s_near, axis=0, keepdims=True))
    e_far = jnp.exp2(s_far - mx)
    e_near = jnp.exp2(s_near - mx)
    den = jnp.sum(e_far, axis=0, keepdims=True) + jnp.sum(e_near, axis=0, keepdims=True)
    inv = jnp.where(t0 + lane_i >= CMP_BLOCK - 1, 1.0 / den, 0.0)
    p_scr[...] = e_far * inv
    p_scr[pl.ds(near0, CMP_NEAR), :] = e_near * inv
    p = p_scr[...]
    o_c = vt_dot(vc_ref[0, 0, 0].astype(MXU_DTYPE), p)

    psum = p[:, 0:Q_BLOCK]
    for r in range(1, HEADS_PER_GROUP):
        psum = psum + p[:, r * Q_BLOCK:(r + 1) * Q_BLOCK]
    p_hi = psum.astype(MXU_DTYPE)
    p_lo = (psum - p_hi.astype(F32)).astype(MXU_DTYPE)
    selt = selt_ref[...]
    imp = (jnp.dot(selt, p_hi, preferred_element_type=F32)
           + jnp.dot(selt, p_lo, preferred_element_type=F32))
    jidx = jax.lax.broadcasted_iota(jnp.int32, (n_slc, Q_BLOCK), 0).astype(F32)
    iidx = jax.lax.broadcasted_iota(jnp.int32, (n_slc, Q_BLOCK), 1)
    cur = (qb * (Q_BLOCK // SLC_BLOCK) + jax.lax.shift_right_logical(iidx, int(math.log2(SLC_BLOCK)))).astype(F32)
    vis = jidx <= cur
    forced = jnp.where(jidx == 0.0, 1.0, jnp.where(jidx == cur, 1.0, jnp.where(jidx == cur - 1.0, 1.0, 0.0)))
    score = jnp.where(vis, jnp.where(forced > 0.5, FORCE, imp), -1.0)
    rem = score
    chosen = jnp.zeros((n_slc, Q_BLOCK), F32)
    for _ in range(top_k):
        best = jnp.max(rem, axis=0, keepdims=True)
        first = jnp.min(jnp.where(rem == best, jidx, float(n_slc)), axis=0, keepdims=True)
        hit = jidx == first
        chosen = jnp.where(hit, 1.0, chosen)
        rem = jnp.where(hit, -2.0, rem)
    n_far = jnp.maximum(qb - 1, 0)
    far_block = jidx < (n_far * (Q_BLOCK // SLC_BLOCK)).astype(F32)
    selb = jnp.where(vis, jnp.where(chosen > 0.5, 0.0, NEG), NEG)
    selbf = jnp.where(far_block, selb, NEG)
    selb_scr[...] = jnp.concatenate([selb] * HEADS_PER_GROUP, axis=1)
    selbf_scr[...] = jnp.concatenate([selbf] * HEADS_PER_GROUP, axis=1) + c31

    near_tile = jnp.maximum(qb - 1, 0)
    m0, l0, acc0 = chunk_softmax(keys(ks_ref, near_tile, 2), keys(vs_ref, near_tile, 2),
                                 block_mask(selb_scr, near_tile, 2) + bnear_ref[0, 0])
    m_scr[...] = m0
    l_scr[...] = l0
    acc_scr[...] = acc0

    def far_chunk(c, carry):
        m_c, l_c, acc_c = chunk_softmax(keys(ks_ref, c * FAR_TILES, FAR_TILES),
                                        keys(vs_ref, c * FAR_TILES, FAR_TILES),
                                        block_mask(selbf_scr, c * FAR_TILES, FAR_TILES))
        m_old = m_scr[...]
        m_new = jnp.maximum(m_old, m_c)
        a_old = jnp.exp2(m_old - m_new)
        a_c = jnp.exp2(m_c - m_new)
        l_scr[...] = a_old * l_scr[...] + a_c * l_c
        acc_scr[...] = a_old * acc_scr[...] + a_c * acc_c
        m_scr[...] = m_new
        return carry

    jax.lax.fori_loop(0, (n_far + FAR_TILES - 1) // FAR_TILES, far_chunk, 0)
    o_s = acc_scr[...] * (1.0 / l_scr[...])

    gates = jax.nn.sigmoid(g_ref[0, 0, 0])
    o = gates[0:1, :] * o_c + gates[1:2, :] * o_s + gates[2:3, :] * o_w
    for r in range(HEADS_PER_GROUP):
        o_ref[0, :, r * HEAD_DIM:(r + 1) * HEAD_DIM] = o[:, r * Q_BLOCK:(r + 1) * Q_BLOCK].T.astype(o_ref.dtype)


def _nsa(att, cmp_pad, selt, kv_col, gt, bnear, bwin, bc, c31):
    b, seq, _ = att.shape
    g = cmp_pad.shape[2]
    nq = seq // Q_BLOCK
    ncp = cmp_pad.shape[3]
    n_slc = selt.shape[0]
    kv_block = lambda which: pl.BlockSpec((1, seq, HEAD_DIM), lambda i, j, k: (i, 0, kv_col + which * g + j))
    cmp_block = lambda which: pl.BlockSpec((1, 1, 1, ncp, HEAD_DIM), lambda i, j, k: (i, which, j, 0, 0))
    top_k = min(SLC_TOPK, n_slc)
    assert nq % FAR_TILES == 0 and nq > WINDOW // Q_BLOCK
    per_g = lambda rows: pl.BlockSpec((1, rows, QL), lambda i, j, k: (j, 0, 0))
    variant = lambda a: pl.BlockSpec((1, 1) + a.shape[2:], lambda i, j, k: (j, jnp.minimum(k, a.shape[1] - 1), 0, 0))
    return pl.pallas_call(
        functools.partial(_nsa_kernel, top_k=top_k),
        grid=(b, g, nq),
        in_specs=[pl.BlockSpec((1, Q_BLOCK, GROUP_WIDTH), lambda i, j, k: (i, k, j)),
                  cmp_block(0), cmp_block(1),
                  pl.BlockSpec((n_slc, ncp), lambda i, j, k: (0, 0)),
                  kv_block(0), kv_block(1), kv_block(2), kv_block(3),
                  pl.BlockSpec((1, 1, 1, 8, QL), lambda i, j, k: (i, j, k, 0, 0)),
                  variant(bnear), variant(bwin), per_g(CMP_NEAR), per_g(8)],
        out_specs=pl.BlockSpec((1, Q_BLOCK, GROUP_WIDTH), lambda i, j, k: (i, k, j)),
        out_shape=jax.ShapeDtypeStruct((b, seq, g * GROUP_WIDTH), MXU_DTYPE),
        scratch_shapes=[pltpu.VMEM((ncp, QL), F32), pltpu.VMEM((n_slc, QL), F32), pltpu.VMEM((n_slc, QL), F32),
                        pltpu.VMEM((1, QL), F32), pltpu.VMEM((1, QL), F32), pltpu.VMEM((HEAD_DIM, QL), F32)],
        compiler_params=_cparams(("arbitrary", "arbitrary", "arbitrary")),
        name="nsa",
    )(att, cmp_pad, cmp_pad, selt, att, att, att, att, gt, bnear, bwin, bc, c31)


def _sgu_kernel(zu_ref, zv_ref, lg_ref, lb_ref, w_ref, bs_ref, o_ref):
    u = jax.nn.gelu(zu_ref[0].astype(F32))
    v = jax.nn.gelu(zv_ref[0].astype(F32))
    mu = jnp.mean(v, axis=-1, keepdims=True)
    var = jnp.mean(jnp.square(v - mu), axis=-1, keepdims=True)
    vn = ((v - mu) * jax.lax.rsqrt(var + EPS) * lg_ref[...] + lb_ref[...]).astype(MXU_DTYPE)
    tc = w_ref.shape[1]
    causal = (jax.lax.broadcasted_iota(jnp.int32, (tc, tc), 0)
              >= jax.lax.broadcasted_iota(jnp.int32, (tc, tc), 1))
    bs = bs_ref[...]
    for h in range(w_ref.shape[0]):
        sl = slice(h * SGU_GROUP_DIM, (h + 1) * SGU_GROUP_DIM)
        w = jnp.where(causal, w_ref[h], 0.0).astype(MXU_DTYPE)
        mixed = jnp.dot(w, vn[:, sl], preferred_element_type=F32) + bs[:, h:h + 1]
        o_ref[0, :, sl] = (u[:, sl] * mixed).astype(o_ref.dtype)


def _sgu(proj3, ln_g, ln_b, w_s, b_st):
    b, t, _ = proj3.shape
    gs, tc, _ = w_s.shape
    vec = pl.BlockSpec((1, SGU_WIDTH), lambda i, j: (0, 0))
    return pl.pallas_call(
        _sgu_kernel,
        grid=(b, t // tc),
        in_specs=[pl.BlockSpec((1, tc, SGU_WIDTH), lambda i, j: (i, j, 0)),
                  pl.BlockSpec((1, tc, SGU_WIDTH), lambda i, j: (i, j, 1)),
                  vec, vec,
                  pl.BlockSpec((gs, tc, tc), lambda i, j: (0, 0, 0)),
                  pl.BlockSpec((tc, gs), lambda i, j: (0, 0))],
        out_specs=pl.BlockSpec((1, tc, SGU_WIDTH), lambda i, j: (i, j, 0)),
        out_shape=jax.ShapeDtypeStruct((b, t, SGU_WIDTH), MXU_DTYPE),
        compiler_params=_cparams(("arbitrary", "arbitrary")),
        name="sgu",
    )(proj3, proj3, ln_g, ln_b, w_s, b_st)


def _merge_kernel(ya_ref, yb_ref, wa_ref, wb_ref, ga_ref, gb_ref, o_ref):
    pa = jnp.dot(ya_ref[...], wa_ref[...], preferred_element_type=F32)
    pb = jnp.dot(yb_ref[...], wb_ref[...], preferred_element_type=F32)
    merged = (jax.nn.sigmoid(ga_ref[...].astype(F32)) * pa + jax.nn.sigmoid(gb_ref[...].astype(F32)) * pb)
    o_ref[...] = merged.astype(o_ref.dtype)


def _merge(ya, yb, wa, wb, proj, ga_col, gb_col, *, tm=1024, tn=512):
    m, ka = ya.shape
    kb = yb.shape[1]
    n = wa.shape[1]
    return pl.pallas_call(
        _merge_kernel,
        grid=(m // tm, n // tn),
        in_specs=[pl.BlockSpec((tm, ka), lambda i, j: (i, 0)),
                  pl.BlockSpec((tm, kb), lambda i, j: (i, 0)),
                  pl.BlockSpec((ka, tn), lambda i, j: (0, j)),
                  pl.BlockSpec((kb, tn), lambda i, j: (0, j)),
                  pl.BlockSpec((tm, tn), lambda i, j: (i, ga_col + j)),
                  pl.BlockSpec((tm, tn), lambda i, j: (i, gb_col + j))],
        out_specs=pl.BlockSpec((tm, tn), lambda i, j: (i, j)),
        out_shape=jax.ShapeDtypeStruct((m, n), MXU_DTYPE),
        compiler_params=_cparams(("arbitrary", "arbitrary")),
        name="merge",
    )(ya, yb, wa, wb, proj, proj)


def _ffn_up_kernel(h_ref, halo_ref, wg_ref, wu_ref, cwg_ref, cwu_ref, cbg_ref, cbu_ref, o_ref, *, seq, sub):
    i = pl.program_id(0)
    tm = h_ref.shape[0]
    wg = wg_ref[...]
    wu = wu_ref[...]
    row = jax.lax.broadcasted_iota(jnp.int32, (sub, wg.shape[1]), 0)

    def conv(a, before, cw_ref, cb_ref):
        a1 = jnp.where(row == 0, before[7:8, :], pltpu.roll(a, 1, 0))
        a2 = jnp.where(row == 0, before[6:7, :], jnp.where(row == 1, before[7:8, :], pltpu.roll(a, 2, 0)))
        return cb_ref[0:1, :] + a2 * cw_ref[0:1, :] + a1 * cw_ref[1:2, :] + a * cw_ref[2:3, :]

    keep = jnp.where((i * tm) % seq == 0, 0.0, 1.0)
    before_g = jnp.dot(halo_ref[...], wg, preferred_element_type=F32) * keep
    before_u = jnp.dot(halo_ref[...], wu, preferred_element_type=F32) * keep
    for s in range(tm // sub):
        hs = h_ref[s * sub:(s + 1) * sub, :]
        ag = jnp.dot(hs, wg, preferred_element_type=F32)
        au = jnp.dot(hs, wu, preferred_element_type=F32)
        gate = conv(ag, before_g, cwg_ref, cbg_ref)
        up = conv(au, before_u, cwu_ref, cbu_ref)
        o_ref[s * sub:(s + 1) * sub, :] = (gate * jax.nn.sigmoid(gate) * up).astype(o_ref.dtype)
        before_g = ag[sub - 8:, :]
        before_u = au[sub - 8:, :]


def _ffn_up(h, w, cw8, cb8, *, seq, tm=2048, sub=512):
    m, d = h.shape
    f = w.shape[1] // 2
    tn = FFN_HALF_TILE
    nj = f // tn
    halo_blocks = tm // 8
    gate_cols = lambda rows: pl.BlockSpec((rows, tn), lambda i, j: (0, j))
    up_cols = lambda rows: pl.BlockSpec((rows, tn), lambda i, j: (0, nj + j))
    return pl.pallas_call(
        functools.partial(_ffn_up_kernel, seq=seq, sub=sub),
        grid=(m // tm, nj),
        in_specs=[pl.BlockSpec((tm, d), lambda i, j: (i, 0)),
                  pl.BlockSpec((8, d), lambda i, j: (jnp.maximum(i * halo_blocks - 1, 0), 0)),
                  gate_cols(d), up_cols(d), gate_cols(8), up_cols(8), gate_cols(8), up_cols(8)],
        out_specs=pl.BlockSpec((tm, tn), lambda i, j: (i, j)),
        out_shape=jax.ShapeDtypeStruct((m, f), MXU_DTYPE),
        compiler_params=_cparams(("arbitrary", "arbitrary")),
        name="ffn_up",
    )(h, h, w, w, cw8, cw8, cb8, cb8)


def _pad_rows(a, rows):
    return jnp.pad(a, ((0, rows - a.shape[0]), (0, 0)))


def _mixer_branches(h, w_in, cmp_pos, cmp_w1, cmp_b1, cmp_w2, cmp_b2, rel_table,
                    sgu_ln_g, sgu_ln_b, sgu_w, sgu_b):
    b, t, d = h.shape
    g, r, dk = NSA_KV_GROUPS, HEADS_PER_GROUP, HEAD_DIM
    m = b * t
    nq = t // Q_BLOCK
    n_cmp = (t - CMP_BLOCK) // CMP_STRIDE + 1
    n_slc = t // SLC_BLOCK
    rows16 = t // CMP_STRIDE
    ncp = rows16 + LANES

    o_gn = NSA_WIDTH + 6 * KV_WIDTH
    o_z = o_gn + NSA_HEADS * 3
    w_att = w_in[:, :o_gn].astype(MXU_DTYPE)
    w_gn = jnp.pad(w_in[:, o_gn:o_z], ((0, 0), (0, LANES - NSA_HEADS * 3))).astype(MXU_DTYPE)
    w_rest = w_in[:, o_z:].astype(MXU_DTYPE)
    c_ga, c_gb = 2 * SGU_WIDTH, 2 * SGU_WIDTH + d
    c_kv = [NSA_WIDTH + i * KV_WIDTH for i in range(6)]
    col_scale = jnp.ones((1, o_gn), F32).at[:, :NSA_WIDTH].set(HEAD_DIM ** -0.5 * LOG2E)

    h2d = h.reshape(m, d)
    proj = _matmul(h2d, w_rest, tm=1024, tn=1024, tk=d, out_dtype=MXU_DTYPE, name="proj_rest")
    patt = _matmul(h2d, w_att, tm=1024, tn=1024, tk=d, out_dtype=MXU_DTYPE, col_scale=col_scale, name="proj_att")
    gn = _matmul(h2d, w_gn, tm=1024, tn=LANES, tk=d, out_dtype=F32, name="proj_gates")
    proj3 = proj.reshape(b, t, proj.shape[1])
    patt3 = patt.reshape(b, t, o_gn)

    kv_c = patt3[:, :, c_kv[0]:c_kv[2]].reshape(b, rows16, CMP_STRIDE, 2, g, dk)
    a_cmp = kv_c.transpose(0, 3, 4, 1, 2, 5).reshape(b, 2, g, rows16, CMP_STRIDE * dk)
    pos8 = jnp.broadcast_to(cmp_pos.reshape(2, 1, CMP_BLOCK * dk), (2, 8, CMP_BLOCK * dk))
    cmp_out = _compress(a_cmp, cmp_w1.astype(MXU_DTYPE), pos8, cmp_b1.reshape(2, 1, CMP_HIDDEN),
                        cmp_w2.astype(MXU_DTYPE), cmp_b2.reshape(2, 1, dk), n_cmp)
    cmp_pad = jnp.pad(cmp_out, ((0, 0), (0, 0), (0, 0), (CMP_FRONT, ncp - CMP_FRONT - rows16), (0, 0)))

    gt = gn[:, :NSA_HEADS * 3].reshape(b, nq, Q_BLOCK, g, r, 3).transpose(0, 3, 1, 5, 4, 2).reshape(b, g, nq, 3, QL)
    gt = jnp.pad(gt, ((0, 0), (0, 0), (0, 0), (0, 5), (0, 0)))
    bwin_full, bc, c31 = _bias_tiles(rel_table)
    bwin, bnear = _bias_variants(bwin_full)
    selt = jnp.asarray(_selection_map_t(n_cmp, n_slc, ncp), MXU_DTYPE)
    y_a = _nsa(patt3, cmp_pad, selt, c_kv[2] // dk, gt, bnear, bwin, bc, c31)

    y_b = _sgu(proj3, sgu_ln_g.reshape(1, SGU_WIDTH), sgu_ln_b.reshape(1, SGU_WIDTH), sgu_w, sgu_b.T)
    return proj, y_a, y_b, c_ga, c_gb


def _token_mixer(h, w_in, cmp_pos, cmp_w1, cmp_b1, cmp_w2, cmp_b2, rel_table,
                 sgu_ln_g, sgu_ln_b, sgu_w, sgu_b, w_proj_nsa, w_proj_sgu):
    proj, y_a, y_b, c_ga, c_gb = _mixer_branches(h, w_in, cmp_pos, cmp_w1, cmp_b1, cmp_w2, cmp_b2, rel_table,
                                                 sgu_ln_g, sgu_ln_b, sgu_w, sgu_b)
    m = proj.shape[0]
    tn = 512
    return _merge(y_a.reshape(m, NSA_WIDTH), y_b.reshape(m, SGU_WIDTH),
                  w_proj_nsa.astype(MXU_DTYPE), w_proj_sgu.astype(MXU_DTYPE),
                  proj, c_ga // tn, c_gb // tn, tm=1024, tn=tn)


def kernel(x, c, w_mod, b_mod, g_norms, w_in, cmp_pos, cmp_w1, cmp_b1, cmp_w2, cmp_b2, rel_table, sgu_ln_g, sgu_ln_b, sgu_w, sgu_b, w_proj_nsa, w_proj_sgu, w_out, w_ffn_up, ffn_conv_w, ffn_conv_b, w_ffn_down):
    b, t, d = x.shape
    m = b * t
    depth = w_mod.shape[0]
    c8 = jnp.pad(c, ((0, 8 - b), (0, 0)))
    for l in range(depth):
        mod = _mod(c8, w_mod[l], b_mod[l].reshape(1, -1))[:b]
        sh1, sc1, gt1, sh2, sc2, gt2 = [v.reshape(b, 1, d) for v in jnp.split(mod, 6, axis=-1)]
        gn = g_norms[l]

        h = _norm_mod(x, gn[0:1], sh1, sc1)
        merged = _token_mixer(h, w_in[l], cmp_pos[l], cmp_w1[l], cmp_b1[l], cmp_w2[l], cmp_b2[l], rel_table,
                              sgu_ln_g[l], sgu_ln_b[l], sgu_w[l], sgu_b[l], w_proj_nsa[l], w_proj_sgu[l])
        y = _matmul(merged, w_out[l].astype(MXU_DTYPE), tm=1024, tn=1024, tk=d, out_dtype=F32, name="out_proj")
        x1, h2 = _resid_norm_mod(x, y.reshape(b, t, d), gn[1:2], gt1, gn[2:3], sh2, sc2)

        cw8 = _pad_rows(ffn_conv_w[l], 8)
        cb8 = _pad_rows(ffn_conv_b[l].reshape(1, -1), 8)
        act = _ffn_up(h2.reshape(m, d), w_ffn_up[l].astype(MXU_DTYPE), cw8, cb8, seq=t)
        d_ff = act.shape[1]
        y2 = _matmul(act, w_ffn_down[l].astype(MXU_DTYPE), tm=512, tn=512, tk=d_ff, out_dtype=F32, name="ffn_down")
        x = _resid_norm(x1, y2.reshape(b, t, d), gn[3:4], gt2)
    return x
```

```python
import functools
import math

import jax
import jax.numpy as jnp
import numpy as np
from jax.experimental import pallas as pl
from jax.experimental.pallas import tpu as pltpu

NSA_HEADS = 16
NSA_KV_GROUPS = 4
HEADS_PER_GROUP = NSA_HEADS // NSA_KV_GROUPS
HEAD_DIM = 128
CMP_BLOCK = 32
CMP_STRIDE = 16
CMP_HIDDEN = 256
SLC_BLOCK = 64
SLC_TOPK = 16
WINDOW = 512
Q_BLOCK = 256
SGU_GROUPS = 16
SGU_GROUP_DIM = 128
SGU_CHUNK = 128
SGU_WIDTH = SGU_GROUPS * SGU_GROUP_DIM
REL_BUCKETS = 32
REL_MAX_DIST = 128
CONV_WIDTH = 3
EPS = 1e-6
NEG = -1e30
FORCE = 1e6
LOG2E = math.log2(math.e)

NSA_WIDTH = NSA_HEADS * HEAD_DIM
KV_WIDTH = NSA_KV_GROUPS * HEAD_DIM
LANES = 128
QL = HEADS_PER_GROUP * Q_BLOCK
GROUP_WIDTH = HEADS_PER_GROUP * HEAD_DIM
CMP_FRONT = 16
CMP_NEAR = 32
FFN_HALF_TILE = 256
FAR_TILES = 4
FAR_PARTS = 2

MXU_DTYPE = jnp.bfloat16
F32 = jnp.float32
VMEM_LIMIT = 56 * 1024 * 1024


def _cparams(sem):
    return pltpu.CompilerParams(dimension_semantics=sem, vmem_limit_bytes=VMEM_LIMIT)


def _mod_kernel(c_ref, w_ref, b_ref, o_ref):
    c = c_ref[...]
    ca = c * jax.nn.sigmoid(c)
    o_ref[...] = jnp.dot(ca.astype(MXU_DTYPE), w_ref[...].astype(MXU_DTYPE),
                         preferred_element_type=F32) + b_ref[...]


def _mod(c8, w_mod, b_mod, tn=512):
    rows, d = c8.shape
    n = w_mod.shape[1]
    return pl.pallas_call(
        _mod_kernel,
        grid=(n // tn,),
        in_specs=[pl.BlockSpec((rows, d), lambda j: (0, 0)),
                  pl.BlockSpec((d, tn), lambda j: (0, j)),
                  pl.BlockSpec((1, tn), lambda j: (0, j))],
        out_specs=pl.BlockSpec((rows, tn), lambda j: (0, j)),
        out_shape=jax.ShapeDtypeStruct((rows, n), F32),
        compiler_params=_cparams(("arbitrary",)),
        name="mod",
    )(c8, w_mod, b_mod)


def _rms(x, g):
    return x * jax.lax.rsqrt(jnp.mean(x * x, axis=-1, keepdims=True) + EPS) * g


def _norm_mod_kernel(x_ref, g_ref, sh_ref, sc_ref, o_ref):
    y = _rms(x_ref[0], g_ref[...])
    o_ref[0] = (y * (1.0 + sc_ref[0]) + sh_ref[0]).astype(o_ref.dtype)


def _norm_mod(x, g, sh, sc, tm=256):
    b, t, d = x.shape
    row = pl.BlockSpec((1, 1, d), lambda i, j: (i, 0, 0))
    return pl.pallas_call(
        _norm_mod_kernel,
        grid=(b, t // tm),
        in_specs=[pl.BlockSpec((1, tm, d), lambda i, j: (i, j, 0)),
                  pl.BlockSpec((1, d), lambda i, j: (0, 0)), row, row],
        out_specs=pl.BlockSpec((1, tm, d), lambda i, j: (i, j, 0)),
        out_shape=jax.ShapeDtypeStruct((b, t, d), MXU_DTYPE),
        compiler_params=_cparams(("arbitrary", "arbitrary")),
        name="norm_mod",
    )(x, g, sh, sc)


def _resid_norm_mod_kernel(x_ref, y_ref, g1_ref, gt_ref, g2_ref, sh_ref, sc_ref, x1_ref, h_ref):
    x1 = x_ref[0] + gt_ref[0] * _rms(y_ref[0].astype(F32), g1_ref[...])
    x1_ref[0] = x1
    h_ref[0] = (_rms(x1, g2_ref[...]) * (1.0 + sc_ref[0]) + sh_ref[0]).astype(h_ref.dtype)


def _resid_norm_mod(x, y, g1, gt, g2, sh, sc, tm=256):
    b, t, d = x.shape
    row = pl.BlockSpec((1, 1, d), lambda i, j: (i, 0, 0))
    gspec = pl.BlockSpec((1, d), lambda i, j: (0, 0))
    blk = pl.BlockSpec((1, tm, d), lambda i, j: (i, j, 0))
    return pl.pallas_call(
        _resid_norm_mod_kernel,
        grid=(b, t // tm),
        in_specs=[blk, blk, gspec, row, gspec, row, row],
        out_specs=[blk, blk],
        out_shape=[jax.ShapeDtypeStruct((b, t, d), F32), jax.ShapeDtypeStruct((b, t, d), MXU_DTYPE)],
        compiler_params=_cparams(("arbitrary", "arbitrary")),
        name="resid_norm_mod",
    )(x, y, g1, gt, g2, sh, sc)


def _resid_norm_kernel(x_ref, y_ref, g_ref, gt_ref, o_ref):
    o_ref[0] = x_ref[0] + gt_ref[0] * _rms(y_ref[0].astype(F32), g_ref[...])


def _resid_norm(x, y, g, gt, tm=256):
    b, t, d = x.shape
    blk = pl.BlockSpec((1, tm, d), lambda i, j: (i, j, 0))
    return pl.pallas_call(
        _resid_norm_kernel,
        grid=(b, t // tm),
        in_specs=[blk, blk, pl.BlockSpec((1, d), lambda i, j: (0, 0)),
                  pl.BlockSpec((1, 1, d), lambda i, j: (i, 0, 0))],
        out_specs=blk,
        out_shape=jax.ShapeDtypeStruct((b, t, d), F32),
        compiler_params=_cparams(("arbitrary", "arbitrary")),
        name="resid_norm",
    )(x, y, g, gt)


def _mm_kernel(*refs, nk, scaled):
    if scaled:
        x_ref, w_ref, s_ref, o_ref = refs[:4]
        rest = refs[4:]
    else:
        x_ref, w_ref, o_ref = refs[:3]
        s_ref = None
        rest = refs[3:]

    def finish(acc):
        if scaled:
            acc = acc * s_ref[...]
        o_ref[...] = acc.astype(o_ref.dtype)

    part = jnp.dot(x_ref[...], w_ref[...], preferred_element_type=F32)
    if nk == 1:
        finish(part)
        return
    acc_ref, = rest
    k = pl.program_id(2)

    @pl.when(k == 0)
    def _():
        acc_ref[...] = jnp.zeros(acc_ref.shape, F32)

    acc_ref[...] += part

    @pl.when(k == nk - 1)
    def _():
        finish(acc_ref[...])


def _matmul(x, w, *, tm, tn, tk, out_dtype, col_scale=None, name="matmul"):
    m, kdim = x.shape
    n = w.shape[1]
    nk = kdim // tk
    in_specs = [pl.BlockSpec((tm, tk), lambda i, j, k: (i, k)),
                pl.BlockSpec((tk, tn), lambda i, j, k: (k, j))]
    args = [x, w]
    if col_scale is not None:
        in_specs.append(pl.BlockSpec((1, tn), lambda i, j, k: (0, j)))
        args.append(col_scale)
    return pl.pallas_call(
        functools.partial(_mm_kernel, nk=nk, scaled=col_scale is not None),
        grid=(m // tm, n // tn, nk),
        in_specs=in_specs,
        out_specs=pl.BlockSpec((tm, tn), lambda i, j, k: (i, j)),
        out_shape=jax.ShapeDtypeStruct((m, n), out_dtype),
        scratch_shapes=[] if nk == 1 else [pltpu.VMEM((tm, tn), F32)],
        compiler_params=_cparams(("arbitrary", "arbitrary", "arbitrary")),
        name=name,
    )(*args)


def _compress_kernel(a_ref, w1_ref, pos_ref, b1_ref, w2_ref, b2_ref, o_ref, *, n_cmp):
    a = a_ref[0, 0, 0]
    w1 = w1_ref[0]
    half = a.shape[1]
    top = jnp.dot(a, w1[:half], preferred_element_type=F32)
    bot = jnp.dot(a, w1[half:], preferred_element_type=F32)
    rows = a.shape[0]
    pre = top + pltpu.roll(bot, rows - 1, 0)
    pos_term = jnp.dot(pos_ref[0].astype(MXU_DTYPE), w1, preferred_element_type=F32)[0:1]
    hid = jax.nn.gelu(pre + pos_term + b1_ref[0])
    out = jnp.dot(hid.astype(MXU_DTYPE), w2_ref[0], preferred_element_type=F32) + b2_ref[0]
    ridx = jax.lax.broadcasted_iota(jnp.int32, out.shape, 0)
    o_ref[0, 0, 0] = jnp.where(ridx < n_cmp, out, 0.0)


def _compress(a, w1, pos8, b1, w2, b2, n_cmp):
    b, two, g, rows, width = a.shape
    hid = w1.shape[2]
    dk = w2.shape[2]
    return pl.pallas_call(
        functools.partial(_compress_kernel, n_cmp=n_cmp),
        grid=(two, b, g),
        in_specs=[pl.BlockSpec((1, 1, 1, rows, width), lambda s, i, j: (i, s, j, 0, 0)),
                  pl.BlockSpec((1, 2 * width, hid), lambda s, i, j: (s, 0, 0)),
                  pl.BlockSpec((1, 8, 2 * width), lambda s, i, j: (s, 0, 0)),
                  pl.BlockSpec((1, 1, hid), lambda s, i, j: (s, 0, 0)),
                  pl.BlockSpec((1, hid, dk), lambda s, i, j: (s, 0, 0)),
                  pl.BlockSpec((1, 1, dk), lambda s, i, j: (s, 0, 0))],
        out_specs=pl.BlockSpec((1, 1, 1, rows, dk), lambda s, i, j: (i, s, j, 0, 0)),
        out_shape=jax.ShapeDtypeStruct((b, two, g, rows, dk), F32),
        compiler_params=_cparams(("arbitrary", "arbitrary", "arbitrary")),
        name="compress",
    )(a, w1, pos8, b1, w2, b2)


def _rel_bucket_np(dist):
    n = np.maximum(dist, 0)
    max_exact = REL_BUCKETS // 2
    nf = np.maximum(n, 1).astype(np.float32)
    large = max_exact + (np.log(nf / max_exact) / math.log(REL_MAX_DIST / max_exact)
                         * (REL_BUCKETS - max_exact)).astype(np.int32)
    large = np.minimum(large, REL_BUCKETS - 1)
    return np.where(n < max_exact, n, large).astype(np.int32)


def _bias_index_tiles():
    w = np.arange(WINDOW + Q_BLOCK)[:, None]
    i = np.arange(Q_BLOCK)[None, :]
    d_w = i - w + WINDOW
    win = np.where((d_w >= 0) & (d_w < WINDOW), _rel_bucket_np(d_w), -1)
    m = np.arange(CMP_NEAR)[:, None]
    d_c = i - CMP_STRIDE * m + (CMP_STRIDE * CMP_FRONT - CMP_BLOCK + 1)
    near = np.where(d_c >= 0, _rel_bucket_np(d_c), -1)
    return win.astype(np.int32), near.astype(np.int32)


def _bias_kernel(tab_ref, idw_ref, idc_ref, ow_ref, os_ref, oc_ref, o31_ref):
    h = pl.program_id(0)
    far = tab_ref[REL_BUCKETS - 1, h] * LOG2E

    def lut(idx, shift):
        out = jnp.full(idx.shape, NEG, F32)
        for b in range(REL_BUCKETS):
            out = jnp.where(idx == b, tab_ref[b, h] * LOG2E - shift, out)
        return out

    ow_ref[0] = lut(idw_ref[...], 0.0)
    os_ref[0] = lut(idw_ref[...], far)
    oc_ref[0] = lut(idc_ref[...], 0.0)
    o31_ref[0] = jnp.full(o31_ref.shape[1:], far, F32)


def _bias_tiles(rel_table):
    win, near = _bias_index_tiles()
    g, r = NSA_KV_GROUPS, HEADS_PER_GROUP
    full = lambda a: pl.BlockSpec(a.shape, lambda h: (0, 0))
    head = lambda rows: pl.BlockSpec((1, rows, Q_BLOCK), lambda h: (h // r, 0, h % r))
    return pl.pallas_call(
        _bias_kernel,
        grid=(NSA_HEADS,),
        in_specs=[pl.BlockSpec(memory_space=pltpu.SMEM), full(win), full(near)],
        out_specs=[head(win.shape[0]), head(win.shape[0]), head(CMP_NEAR), head(8)],
        out_shape=[jax.ShapeDtypeStruct((g, win.shape[0], QL), F32), jax.ShapeDtypeStruct((g, win.shape[0], QL), F32),
                   jax.ShapeDtypeStruct((g, CMP_NEAR, QL), F32), jax.ShapeDtypeStruct((g, 8, QL), F32)],
        compiler_params=_cparams(("arbitrary",)),
        name="bias_tiles",
    )(rel_table, jnp.asarray(win), jnp.asarray(near))


def _bias_variants(bwin_full, n_tiles):
    g = bwin_full.shape[0]
    n_win = WINDOW // Q_BLOCK
    tiles = bwin_full.reshape(g, n_win + 1, Q_BLOCK, QL)
    masked = jnp.full((g, Q_BLOCK, QL), NEG, F32)
    out = []
    for v in range(n_tiles):
        chunk = [tiles[:, n_win - v + u] if v - u >= 0 else masked for u in range(n_tiles)]
        out.append(jnp.concatenate(chunk, axis=1))
    return jnp.stack(out, axis=1)


def _selection_map_t(n_cmp, n_slc, ncp):
    ratio = SLC_BLOCK // CMP_STRIDE
    span = CMP_BLOCK // CMP_STRIDE
    d = np.arange(n_cmp)[:, None] - ratio * np.arange(n_slc)[None, :]
    a = d[..., None] + np.arange(span)
    sel = np.sum((a >= 0) & (a < ratio), axis=-1).astype(np.float32)
    out = np.zeros((n_slc, ncp), np.float32)
    out[:, CMP_FRONT:CMP_FRONT + n_cmp] = sel.T
    return out


def _nsa_kernel(q_ref, kc_ref, vc_ref, selt_ref, ks_ref, vs_ref, kw_ref, vw_ref, g_ref,
                bnear_ref, bwin_ref, bc_ref, c31_ref, onehot_ref, o_ref,
                p_scr, selb_scr, selbf_scr, m_scr, l_scr, acc_scr, *, top_k):
    qb = pl.program_id(2)
    qn = q_ref[0]
    q = jnp.concatenate([qn[:, r * HEAD_DIM:(r + 1) * HEAD_DIM].astype(F32).T for r in range(HEADS_PER_GROUP)],
                        axis=1).astype(MXU_DTYPE)
    ncp = kc_ref.shape[3]
    n_slc = selt_ref.shape[0]
    c31 = c31_ref[0][0:1, :]
    lane_i = jax.lax.broadcasted_iota(jnp.int32, (1, QL), 1) & (Q_BLOCK - 1)
    t0 = qb * Q_BLOCK

    def vt_dot(v_rows, p):
        return jax.lax.dot_general(v_rows, p.astype(MXU_DTYPE), (((0,), (0,)), ((), ())),
                                   preferred_element_type=F32)

    def scores_softmax(s, v_rows):
        m_c = jnp.max(s, axis=0, keepdims=True)
        pe = jnp.exp2(s - m_c)
        return m_c, jnp.sum(pe, axis=0, keepdims=True), vt_dot(v_rows, pe)

    def chunk_softmax(k_rows, v_rows, bias):
        return scores_softmax(jnp.dot(k_rows, q, preferred_element_type=F32) + bias, v_rows)

    def keys(ref, tile, n_tiles):
        return ref[0, pl.ds(pl.multiple_of(tile * Q_BLOCK, Q_BLOCK), n_tiles * Q_BLOCK), :]

    def block_mask(scr, tile, n_tiles):
        blocks_per_tile = Q_BLOCK // SLC_BLOCK
        rows = [jnp.broadcast_to(scr[pl.ds(blocks_per_tile * tile + u, 1), :], (SLC_BLOCK, QL))
                for u in range(blocks_per_tile * n_tiles)]
        return jnp.concatenate(rows, axis=0)

    n_win = WINDOW // Q_BLOCK
    win_tile = jnp.maximum(qb - n_win, 0)
    near_tile = jnp.maximum(qb - 1, 0)
    s_all = jnp.dot(kc_ref[0, 0, 0].astype(MXU_DTYPE), q, preferred_element_type=F32)
    s_win = jnp.dot(keys(kw_ref, win_tile, n_win + 1), q, preferred_element_type=F32) + bwin_ref[0, 0]
    s_pair = jnp.dot(keys(ks_ref, near_tile, 2), q, preferred_element_type=F32) + bnear_ref[0, 0]

    prow = jax.lax.broadcasted_iota(jnp.int32, (ncp, QL), 0)
    near0 = pl.multiple_of(qb * (Q_BLOCK // CMP_STRIDE), 8)
    s_far = jnp.where(prow < near0, jnp.where(prow >= CMP_FRONT, s_all + c31, NEG), NEG)
    kn = kc_ref[0, 0, 0, pl.ds(near0, CMP_NEAR), :]
    mrow = jax.lax.broadcasted_iota(jnp.int32, (CMP_NEAR, QL), 0)
    s_near = jnp.dot(kn.astype(MXU_DTYPE), q, preferred_element_type=F32) + bc_ref[0]
    s_near = jnp.where(mrow >= CMP_FRONT - near0, s_near, NEG)
    mx = jnp.maximum(jnp.max(s_far, axis=0, keepdims=True), jnp.max(s_near, axis=0, keepdims=True))
    e_far = jnp.exp2(s_far - mx)
    e_near = jnp.exp2(s_near - mx)
    den = jnp.sum(e_far, axis=0, keepdims=True) + jnp.sum(e_near, axis=0, keepdims=True)
    inv = jnp.where(t0 + lane_i >= CMP_BLOCK - 1, 1.0 / den, 0.0)
    p_scr[...] = e_far * inv
    p_scr[pl.ds(near0, CMP_NEAR), :] = e_near * inv
    p = p_scr[...]
    o_c = vt_dot(vc_ref[0, 0, 0].astype(MXU_DTYPE), p)

    psum = p[:, 0:Q_BLOCK]
    for r in range(1, HEADS_PER_GROUP):
        psum = psum + p[:, r * Q_BLOCK:(r + 1) * Q_BLOCK]
    p_hi = psum.astype(MXU_DTYPE)
    p_lo = (psum - p_hi.astype(F32)).astype(MXU_DTYPE)
    selt = selt_ref[...]
    imp = (jnp.dot(selt, p_hi, preferred_element_type=F32)
           + jnp.dot(selt, p_lo, preferred_element_type=F32))

    _, l_w, acc_w = scores_softmax(s_win, keys(vw_ref, win_tile, n_win + 1))
    o_w = acc_w * (1.0 / l_w)

    jidx = jax.lax.broadcasted_iota(jnp.int32, (n_slc, Q_BLOCK), 0).astype(F32)
    iidx = jax.lax.broadcasted_iota(jnp.int32, (n_slc, Q_BLOCK), 1)
    cur = (qb * (Q_BLOCK // SLC_BLOCK) + jax.lax.shift_right_logical(iidx, int(math.log2(SLC_BLOCK)))).astype(F32)
    vis = jidx <= cur
    forced = jnp.where(jidx == 0.0, 1.0, jnp.where(jidx == cur, 1.0, jnp.where(jidx == cur - 1.0, 1.0, 0.0)))
    score = jnp.where(vis, jnp.where(forced > 0.5, FORCE, imp), -1.0)
    rem = score
    chosen = jnp.zeros((n_slc, Q_BLOCK), F32)
    for _ in range(top_k):
        best = jnp.max(rem, axis=0, keepdims=True)
        first = jnp.min(jnp.where(rem == best, jidx, float(n_slc)), axis=0, keepdims=True)
        hit = jidx == first
        chosen = jnp.where(hit, 1.0, chosen)
        rem = jnp.where(hit, -2.0, rem)
    n_far = jnp.maximum(qb - 1, 0)
    far_block = jidx < (n_far * (Q_BLOCK // SLC_BLOCK)).astype(F32)
    selb = jnp.where(vis, jnp.where(chosen > 0.5, 0.0, NEG), NEG)
    selbf = jnp.where(far_block, selb, NEG)
    selb_scr[...] = jnp.concatenate([selb] * HEADS_PER_GROUP, axis=1)
    selbf_scr[...] = jnp.concatenate([selbf] * HEADS_PER_GROUP, axis=1)

    m0, l0, acc0 = scores_softmax(s_pair + block_mask(selb_scr, near_tile, 2), keys(vs_ref, near_tile, 2))
    m_scr[...] = m0
    l_scr[...] = l0
    acc_scr[...] = acc0

    far_blocks = FAR_TILES * Q_BLOCK // SLC_BLOCK
    pad_rows = jnp.zeros((HEAD_DIM - far_blocks, QL), MXU_DTYPE)

    def far_chunk(c, carry):
        mask_rows = selbf_scr[pl.ds(pl.multiple_of(c * far_blocks, far_blocks), far_blocks), :]
        q_aug = jnp.concatenate([q, mask_rows.astype(MXU_DTYPE), pad_rows], axis=0)
        part = FAR_TILES // FAR_PARTS
        scores = []
        for u in range(FAR_PARTS):
            k_aug = jnp.concatenate([keys(ks_ref, c * FAR_TILES + u * part, part),
                                     onehot_ref[u * part * Q_BLOCK:(u + 1) * part * Q_BLOCK, :]], axis=1)
            scores.append(jnp.dot(k_aug, q_aug, preferred_element_type=F32))
        stats = [scores_softmax(scores[u], keys(vs_ref, c * FAR_TILES + u * part, part)) for u in range(FAR_PARTS)]
        m_old = m_scr[...]
        m_new = m_old
        for m_c, _, _ in stats:
            m_new = jnp.maximum(m_new, m_c)
        a_old = jnp.exp2(m_old - m_new)
        l_new = a_old * l_scr[...]
        acc_new = a_old * acc_scr[...]
        for m_c, l_c, acc_c in stats:
            a_c = jnp.exp2(m_c - m_new)
            l_new = l_new + a_c * l_c
            acc_new = acc_new + a_c * acc_c
        l_scr[...] = l_new
        acc_scr[...] = acc_new
        m_scr[...] = m_new
        return carry

    jax.lax.fori_loop(0, (n_far + FAR_TILES - 1) // FAR_TILES, far_chunk, 0)
    o_s = acc_scr[...] * (1.0 / l_scr[...])

    gates = jax.nn.sigmoid(g_ref[0, 0, 0])
    o = gates[0:1, :] * o_c + gates[1:2, :] * o_s + gates[2:3, :] * o_w
    for r in range(HEADS_PER_GROUP):
        o_ref[0, :, r * HEAD_DIM:(r + 1) * HEAD_DIM] = o[:, r * Q_BLOCK:(r + 1) * Q_BLOCK].T.astype(o_ref.dtype)


def _nsa(att, cmp_pad, selt, kv_col, gt, bnear, bwin, bc, c31):
    b, seq, _ = att.shape
    g = cmp_pad.shape[2]
    nq = seq // Q_BLOCK
    ncp = cmp_pad.shape[3]
    n_slc = selt.shape[0]
    kv_block = lambda which: pl.BlockSpec((1, seq, HEAD_DIM), lambda i, j, k: (i, 0, kv_col + which * g + j))
    cmp_block = lambda which: pl.BlockSpec((1, 1, 1, ncp, HEAD_DIM), lambda i, j, k: (i, which, j, 0, 0))
    top_k = min(SLC_TOPK, n_slc)
    assert nq % FAR_TILES == 0 and nq > WINDOW // Q_BLOCK
    per_g = lambda rows: pl.BlockSpec((1, rows, QL), lambda i, j, k: (j, 0, 0))
    block_of_row = np.arange(FAR_TILES * Q_BLOCK)[:, None] // SLC_BLOCK
    onehot = jnp.asarray(block_of_row == np.arange(HEAD_DIM)[None, :], MXU_DTYPE)
    variant = lambda a: pl.BlockSpec((1, 1) + a.shape[2:], lambda i, j, k: (j, jnp.minimum(k, a.shape[1] - 1), 0, 0))
    return pl.pallas_call(
        functools.partial(_nsa_kernel, top_k=top_k),
        grid=(b, g, nq),
        in_specs=[pl.BlockSpec((1, Q_BLOCK, GROUP_WIDTH), lambda i, j, k: (i, k, j)),
                  cmp_block(0), cmp_block(1),
                  pl.BlockSpec((n_slc, ncp), lambda i, j, k: (0, 0)),
                  kv_block(0), kv_block(1), kv_block(2), kv_block(3),
                  pl.BlockSpec((1, 1, 1, 8, QL), lambda i, j, k: (i, j, k, 0, 0)),
                  variant(bnear), variant(bwin), per_g(CMP_NEAR), per_g(8),
                  pl.BlockSpec(onehot.shape, lambda i, j, k: (0, 0))],
        out_specs=pl.BlockSpec((1, Q_BLOCK, GROUP_WIDTH), lambda i, j, k: (i, k, j)),
        out_shape=jax.ShapeDtypeStruct((b, seq, g * GROUP_WIDTH), MXU_DTYPE),
        scratch_shapes=[pltpu.VMEM((ncp, QL), F32), pltpu.VMEM((n_slc, QL), F32), pltpu.VMEM((n_slc, QL), F32),
                        pltpu.VMEM((1, QL), F32), pltpu.VMEM((1, QL), F32), pltpu.VMEM((HEAD_DIM, QL), F32)],
        compiler_params=_cparams(("arbitrary", "arbitrary", "arbitrary")),
        name="nsa",
    )(att, cmp_pad, cmp_pad, selt, att, att, att, att, gt, bnear, bwin, bc, c31, onehot)


def _sgu_kernel(zu_ref, zv_ref, lg_ref, lb_ref, w_ref, bs_ref, o_ref):
    u = jax.nn.gelu(zu_ref[0].astype(F32))
    v = jax.nn.gelu(zv_ref[0].astype(F32))
    mu = jnp.mean(v, axis=-1, keepdims=True)
    var = jnp.mean(jnp.square(v - mu), axis=-1, keepdims=True)
    vn = ((v - mu) * jax.lax.rsqrt(var + EPS) * lg_ref[...] + lb_ref[...]).astype(MXU_DTYPE)
    tc = w_ref.shape[1]
    causal = (jax.lax.broadcasted_iota(jnp.int32, (tc, tc), 0)
              >= jax.lax.broadcasted_iota(jnp.int32, (tc, tc), 1))
    bs = bs_ref[...]
    for h in range(w_ref.shape[0]):
        sl = slice(h * SGU_GROUP_DIM, (h + 1) * SGU_GROUP_DIM)
        w = jnp.where(causal, w_ref[h], 0.0).astype(MXU_DTYPE)
        mixed = jnp.dot(w, vn[:, sl], preferred_element_type=F32) + bs[:, h:h + 1]
        o_ref[0, :, sl] = (u[:, sl] * mixed).astype(o_ref.dtype)


def _sgu(proj3, ln_g, ln_b, w_s, b_st):
    b, t, _ = proj3.shape
    gs, tc, _ = w_s.shape
    vec = pl.BlockSpec((1, SGU_WIDTH), lambda i, j: (0, 0))
    return pl.pallas_call(
        _sgu_kernel,
        grid=(b, t // tc),
        in_specs=[pl.BlockSpec((1, tc, SGU_WIDTH), lambda i, j: (i, j, 0)),
                  pl.BlockSpec((1, tc, SGU_WIDTH), lambda i, j: (i, j, 1)),
                  vec, vec,
                  pl.BlockSpec((gs, tc, tc), lambda i, j: (0, 0, 0)),
                  pl.BlockSpec((tc, gs), lambda i, j: (0, 0))],
        out_specs=pl.BlockSpec((1, tc, SGU_WIDTH), lambda i, j: (i, j, 0)),
        out_shape=jax.ShapeDtypeStruct((b, t, SGU_WIDTH), MXU_DTYPE),
        compiler_params=_cparams(("arbitrary", "arbitrary")),
        name="sgu",
    )(proj3, proj3, ln_g, ln_b, w_s, b_st)


def _merge_kernel(ya_ref, yb_ref, wa_ref, wb_ref, ga_ref, gb_ref, o_ref):
    pa = jnp.dot(ya_ref[...], wa_ref[...], preferred_element_type=F32)
    pb = jnp.dot(yb_ref[...], wb_ref[...], preferred_element_type=F32)
    merged = (jax.nn.sigmoid(ga_ref[...].astype(F32)) * pa + jax.nn.sigmoid(gb_ref[...].astype(F32)) * pb)
    o_ref[...] = merged.astype(o_ref.dtype)


def _merge(ya, yb, wa, wb, proj, ga_col, gb_col, *, tm=1024, tn=512):
    m, ka = ya.shape
    kb = yb.shape[1]
    n = wa.shape[1]
    return pl.pallas_call(
        _merge_kernel,
        grid=(m // tm, n // tn),
        in_specs=[pl.BlockSpec((tm, ka), lambda i, j: (i, 0)),
                  pl.BlockSpec((tm, kb), lambda i, j: (i, 0)),
                  pl.BlockSpec((ka, tn), lambda i, j: (0, j)),
                  pl.BlockSpec((kb, tn), lambda i, j: (0, j)),
                  pl.BlockSpec((tm, tn), lambda i, j: (i, ga_col + j)),
                  pl.BlockSpec((tm, tn), lambda i, j: (i, gb_col + j))],
        out_specs=pl.BlockSpec((tm, tn), lambda i, j: (i, j)),
        out_shape=jax.ShapeDtypeStruct((m, n), MXU_DTYPE),
        compiler_params=_cparams(("arbitrary", "arbitrary")),
        name="merge",
    )(ya, yb, wa, wb, proj, proj)


def _ffn_up_kernel(h_ref, halo_ref, wg_ref, wu_ref, cwg_ref, cwu_ref, cbg_ref, cbu_ref, o_ref, *, seq, sub):
    i = pl.program_id(0)
    tm = h_ref.shape[0]
    wg = wg_ref[...]
    wu = wu_ref[...]
    row = jax.lax.broadcasted_iota(jnp.int32, (sub, wg.shape[1]), 0)

    def conv(a, before, cw_ref, cb_ref):
        a1 = jnp.where(row == 0, before[7:8, :], pltpu.roll(a, 1, 0))
        a2 = jnp.where(row == 0, before[6:7, :], jnp.where(row == 1, before[7:8, :], pltpu.roll(a, 2, 0)))
        return cb_ref[0:1, :] + a2 * cw_ref[0:1, :] + a1 * cw_ref[1:2, :] + a * cw_ref[2:3, :]

    keep = jnp.where((i * tm) % seq == 0, 0.0, 1.0)
    before_g = jnp.dot(halo_ref[...], wg, preferred_element_type=F32) * keep
    before_u = jnp.dot(halo_ref[...], wu, preferred_element_type=F32) * keep
    for s in range(tm // sub):
        hs = h_ref[s * sub:(s + 1) * sub, :]
        ag = jnp.dot(hs, wg, preferred_element_type=F32)
        au = jnp.dot(hs, wu, preferred_element_type=F32)
        gate = conv(ag, before_g, cwg_ref, cbg_ref)
        up = conv(au, before_u, cwu_ref, cbu_ref)
        o_ref[s * sub:(s + 1) * sub, :] = (gate * jax.nn.sigmoid(gate) * up).astype(o_ref.dtype)
        before_g = ag[sub - 8:, :]
        before_u = au[sub - 8:, :]


def _ffn_up(h, w, cw8, cb8, *, seq, tm=2048, sub=512):
    m, d = h.shape
    f = w.shape[1] // 2
    tn = FFN_HALF_TILE
    nj = f // tn
    halo_blocks = tm // 8
    gate_cols = lambda rows: pl.BlockSpec((rows, tn), lambda i, j: (0, j))
    up_cols = lambda rows: pl.BlockSpec((rows, tn), lambda i, j: (0, nj + j))
    return pl.pallas_call(
        functools.partial(_ffn_up_kernel, seq=seq, sub=sub),
        grid=(m // tm, nj),
        in_specs=[pl.BlockSpec((tm, d), lambda i, j: (i, 0)),
                  pl.BlockSpec((8, d), lambda i, j: (jnp.maximum(i * halo_blocks - 1, 0), 0)),
                  gate_cols(d), up_cols(d), gate_cols(8), up_cols(8), gate_cols(8), up_cols(8)],
        out_specs=pl.BlockSpec((tm, tn), lambda i, j: (i, j)),
        out_shape=jax.ShapeDtypeStruct((m, f), MXU_DTYPE),
        compiler_params=_cparams(("arbitrary", "arbitrary")),
        name="ffn_up",
    )(h, h, w, w, cw8, cw8, cb8, cb8)


def _pad_rows(a, rows):
    return jnp.pad(a, ((0, rows - a.shape[0]), (0, 0)))


def _mixer_branches(h, w_in, cmp_pos, cmp_w1, cmp_b1, cmp_w2, cmp_b2, rel_table,
                    sgu_ln_g, sgu_ln_b, sgu_w, sgu_b):
    b, t, d = h.shape
    g, r, dk = NSA_KV_GROUPS, HEADS_PER_GROUP, HEAD_DIM
    m = b * t
    nq = t // Q_BLOCK
    n_cmp = (t - CMP_BLOCK) // CMP_STRIDE + 1
    n_slc = t // SLC_BLOCK
    rows16 = t // CMP_STRIDE
    ncp = rows16 + LANES

    o_gn = NSA_WIDTH + 6 * KV_WIDTH
    o_z = o_gn + NSA_HEADS * 3
    w_att = w_in[:, :o_gn].astype(MXU_DTYPE)
    w_gn = jnp.pad(w_in[:, o_gn:o_z], ((0, 0), (0, LANES - NSA_HEADS * 3))).astype(MXU_DTYPE)
    w_rest = w_in[:, o_z:].astype(MXU_DTYPE)
    c_ga, c_gb = 2 * SGU_WIDTH, 2 * SGU_WIDTH + d
    c_kv = [NSA_WIDTH + i * KV_WIDTH for i in range(6)]
    col_scale = jnp.ones((1, o_gn), F32).at[:, :NSA_WIDTH].set(HEAD_DIM ** -0.5 * LOG2E)

    h2d = h.reshape(m, d)
    proj = _matmul(h2d, w_rest, tm=1024, tn=1024, tk=d, out_dtype=MXU_DTYPE, name="proj_rest")
    patt = _matmul(h2d, w_att, tm=1024, tn=1024, tk=d, out_dtype=MXU_DTYPE, col_scale=col_scale, name="proj_att")
    gn = _matmul(h2d, w_gn, tm=1024, tn=LANES, tk=d, out_dtype=F32, name="proj_gates")
    proj3 = proj.reshape(b, t, proj.shape[1])
    patt3 = patt.reshape(b, t, o_gn)

    kv_c = patt3[:, :, c_kv[0]:c_kv[2]].reshape(b, rows16, CMP_STRIDE, 2, g, dk)
    a_cmp = kv_c.transpose(0, 3, 4, 1, 2, 5).reshape(b, 2, g, rows16, CMP_STRIDE * dk)
    pos8 = jnp.broadcast_to(cmp_pos.reshape(2, 1, CMP_BLOCK * dk), (2, 8, CMP_BLOCK * dk))
    cmp_out = _compress(a_cmp, cmp_w1.astype(MXU_DTYPE), pos8, cmp_b1.reshape(2, 1, CMP_HIDDEN),
                        cmp_w2.astype(MXU_DTYPE), cmp_b2.reshape(2, 1, dk), n_cmp)
    cmp_pad = jnp.pad(cmp_out, ((0, 0), (0, 0), (0, 0), (CMP_FRONT, ncp - CMP_FRONT - rows16), (0, 0)))

    gt = gn[:, :NSA_HEADS * 3].reshape(b, nq, Q_BLOCK, g, r, 3).transpose(0, 3, 1, 5, 4, 2).reshape(b, g, nq, 3, QL)
    gt = jnp.pad(gt, ((0, 0), (0, 0), (0, 0), (0, 5), (0, 0)))
    bwin_full, bwin_shifted, bc, c31 = _bias_tiles(rel_table)
    bwin = _bias_variants(bwin_full, WINDOW // Q_BLOCK + 1)
    bnear = _bias_variants(bwin_shifted, 2)
    selt = jnp.asarray(_selection_map_t(n_cmp, n_slc, ncp), MXU_DTYPE)
    y_a = _nsa(patt3, cmp_pad, selt, c_kv[2] // dk, gt, bnear, bwin, bc, c31)

    y_b = _sgu(proj3, sgu_ln_g.reshape(1, SGU_WIDTH), sgu_ln_b.reshape(1, SGU_WIDTH), sgu_w, sgu_b.T)
    return proj, y_a, y_b, c_ga, c_gb


def _token_mixer(h, w_in, cmp_pos, cmp_w1, cmp_b1, cmp_w2, cmp_b2, rel_table,
                 sgu_ln_g, sgu_ln_b, sgu_w, sgu_b, w_proj_nsa, w_proj_sgu):
    proj, y_a, y_b, c_ga, c_gb = _mixer_branches(h, w_in, cmp_pos, cmp_w1, cmp_b1, cmp_w2, cmp_b2, rel_table,
                                                 sgu_ln_g, sgu_ln_b, sgu_w, sgu_b)
    m = proj.shape[0]
    tn = 512
    return _merge(y_a.reshape(m, NSA_WIDTH), y_b.reshape(m, SGU_WIDTH),
                  w_proj_nsa.astype(MXU_DTYPE), w_proj_sgu.astype(MXU_DTYPE),
                  proj, c_ga // tn, c_gb // tn, tm=1024, tn=tn)


def kernel(x, c, w_mod, b_mod, g_norms, w_in, cmp_pos, cmp_w1, cmp_b1, cmp_w2, cmp_b2, rel_table, sgu_ln_g, sgu_ln_b, sgu_w, sgu_b, w_proj_nsa, w_proj_sgu, w_out, w_ffn_up, ffn_conv_w, ffn_conv_b, w_ffn_down):
    b, t, d = x.shape
    m = b * t
    depth = w_mod.shape[0]
    c8 = jnp.pad(c, ((0, 8 - b), (0, 0)))
    for l in range(depth):
        mod = _mod(c8, w_mod[l], b_mod[l].reshape(1, -1))[:b]
        sh1, sc1, gt1, sh2, sc2, gt2 = [v.reshape(b, 1, d) for v in jnp.split(mod, 6, axis=-1)]
        gn = g_norms[l]

        h = _norm_mod(x, gn[0:1], sh1, sc1)
        merged = _token_mixer(h, w_in[l], cmp_pos[l], cmp_w1[l], cmp_b1[l], cmp_w2[l], cmp_b2[l], rel_table,
                              sgu_ln_g[l], sgu_ln_b[l], sgu_w[l], sgu_b[l], w_proj_nsa[l], w_proj_sgu[l])
        y = _matmul(merged, w_out[l].astype(MXU_DTYPE), tm=1024, tn=1024, tk=d, out_dtype=MXU_DTYPE, name="out_proj")
        x1, h2 = _resid_norm_mod(x, y.reshape(b, t, d), gn[1:2], gt1, gn[2:3], sh2, sc2)

        cw8 = _pad_rows(ffn_conv_w[l], 8)
        cb8 = _pad_rows(ffn_conv_b[l].reshape(1, -1), 8)
        act = _ffn_up(h2.reshape(m, d), w_ffn_up[l].astype(MXU_DTYPE), cw8, cb8, seq=t)
        d_ff = act.shape[1]
        y2 = _matmul(act, w_ffn_down[l].astype(MXU_DTYPE), tm=512, tn=512, tk=d_ff, out_dtype=MXU_DTYPE,
                     name="ffn_down")
        x = _resid_norm(x1, y2.reshape(b, t, d), gn[3:4], gt2)
    return x
```

```python
import functools
import math

import jax
import jax.numpy as jnp
import numpy as np
from jax.experimental import pallas as pl
from jax.experimental.pallas import tpu as pltpu

NSA_HEADS = 16
NSA_KV_GROUPS = 4
HEADS_PER_GROUP = NSA_HEADS // NSA_KV_GROUPS
HEAD_DIM = 128
CMP_BLOCK = 32
CMP_STRIDE = 16
CMP_HIDDEN = 256
SLC_BLOCK = 64
SLC_TOPK = 16
WINDOW = 512
Q_BLOCK = 256
SGU_GROUPS = 16
SGU_GROUP_DIM = 128
SGU_CHUNK = 128
SGU_WIDTH = SGU_GROUPS * SGU_GROUP_DIM
REL_BUCKETS = 32
REL_MAX_DIST = 128
CONV_WIDTH = 3
EPS = 1e-6
NEG = -1e30
FORCE = 1e6
LOG2E = math.log2(math.e)

NSA_WIDTH = NSA_HEADS * HEAD_DIM
KV_WIDTH = NSA_KV_GROUPS * HEAD_DIM
LANES = 128
QL = HEADS_PER_GROUP * Q_BLOCK
GROUP_WIDTH = HEADS_PER_GROUP * HEAD_DIM
CMP_FRONT = 16
CMP_NEAR = 32
FFN_HALF_TILE = 256
FAR_TILES = 4
FAR_PARTS = 2

MXU_DTYPE = jnp.bfloat16
F32 = jnp.float32
VMEM_LIMIT = 56 * 1024 * 1024


def _cparams(sem):
    return pltpu.CompilerParams(dimension_semantics=sem, vmem_limit_bytes=VMEM_LIMIT)


def _mod_kernel(c_ref, w_ref, b_ref, o_ref):
    c = c_ref[...]
    ca = c * jax.nn.sigmoid(c)
    o_ref[...] = jnp.dot(ca.astype(MXU_DTYPE), w_ref[...].astype(MXU_DTYPE),
                         preferred_element_type=F32) + b_ref[...]


def _mod(c8, w_mod, b_mod, tn=512):
    rows, d = c8.shape
    n = w_mod.shape[1]
    return pl.pallas_call(
        _mod_kernel,
        grid=(n // tn,),
        in_specs=[pl.BlockSpec((rows, d), lambda j: (0, 0)),
                  pl.BlockSpec((d, tn), lambda j: (0, j)),
                  pl.BlockSpec((1, tn), lambda j: (0, j))],
        out_specs=pl.BlockSpec((rows, tn), lambda j: (0, j)),
        out_shape=jax.ShapeDtypeStruct((rows, n), F32),
        compiler_params=_cparams(("arbitrary",)),
        name="mod",
    )(c8, w_mod, b_mod)


def _rms(x, g):
    return x * jax.lax.rsqrt(jnp.mean(x * x, axis=-1, keepdims=True) + EPS) * g


def _norm_mod_kernel(x_ref, g_ref, sh_ref, sc_ref, o_ref):
    y = _rms(x_ref[0], g_ref[...])
    o_ref[0] = (y * (1.0 + sc_ref[0]) + sh_ref[0]).astype(o_ref.dtype)


def _norm_mod(x, g, sh, sc, tm=256):
    b, t, d = x.shape
    row = pl.BlockSpec((1, 1, d), lambda i, j: (i, 0, 0))
    return pl.pallas_call(
        _norm_mod_kernel,
        grid=(b, t // tm),
        in_specs=[pl.BlockSpec((1, tm, d), lambda i, j: (i, j, 0)),
                  pl.BlockSpec((1, d), lambda i, j: (0, 0)), row, row],
        out_specs=pl.BlockSpec((1, tm, d), lambda i, j: (i, j, 0)),
        out_shape=jax.ShapeDtypeStruct((b, t, d), MXU_DTYPE),
        compiler_params=_cparams(("arbitrary", "arbitrary")),
        name="norm_mod",
    )(x, g, sh, sc)


def _resid_norm_mod_kernel(x_ref, y_ref, g1_ref, gt_ref, g2_ref, sh_ref, sc_ref, x1_ref, h_ref):
    x1 = x_ref[0] + gt_ref[0] * _rms(y_ref[0].astype(F32), g1_ref[...])
    x1_ref[0] = x1
    h_ref[0] = (_rms(x1, g2_ref[...]) * (1.0 + sc_ref[0]) + sh_ref[0]).astype(h_ref.dtype)


def _resid_norm_mod(x, y, g1, gt, g2, sh, sc, tm=256):
    b, t, d = x.shape
    row = pl.BlockSpec((1, 1, d), lambda i, j: (i, 0, 0))
    gspec = pl.BlockSpec((1, d), lambda i, j: (0, 0))
    blk = pl.BlockSpec((1, tm, d), lambda i, j: (i, j, 0))
    return pl.pallas_call(
        _resid_norm_mod_kernel,
        grid=(b, t // tm),
        in_specs=[blk, blk, gspec, row, gspec, row, row],
        out_specs=[blk, blk],
        out_shape=[jax.ShapeDtypeStruct((b, t, d), F32), jax.ShapeDtypeStruct((b, t, d), MXU_DTYPE)],
        compiler_params=_cparams(("arbitrary", "arbitrary")),
        name="resid_norm_mod",
    )(x, y, g1, gt, g2, sh, sc)


def _resid_norm_kernel(x_ref, y_ref, g_ref, gt_ref, o_ref):
    o_ref[0] = x_ref[0] + gt_ref[0] * _rms(y_ref[0].astype(F32), g_ref[...])


def _resid_norm(x, y, g, gt, tm=256):
    b, t, d = x.shape
    blk = pl.BlockSpec((1, tm, d), lambda i, j: (i, j, 0))
    return pl.pallas_call(
        _resid_norm_kernel,
        grid=(b, t // tm),
        in_specs=[blk, blk, pl.BlockSpec((1, d), lambda i, j: (0, 0)),
                  pl.BlockSpec((1, 1, d), lambda i, j: (i, 0, 0))],
        out_specs=blk,
        out_shape=jax.ShapeDtypeStruct((b, t, d), F32),
        compiler_params=_cparams(("arbitrary", "arbitrary")),
        name="resid_norm",
    )(x, y, g, gt)


def _mm_kernel(*refs, nk, scaled):
    if scaled:
        x_ref, w_ref, s_ref, o_ref = refs[:4]
        rest = refs[4:]
    else:
        x_ref, w_ref, o_ref = refs[:3]
        s_ref = None
        rest = refs[3:]

    def finish(acc):
        if scaled:
            acc = acc * s_ref[...]
        o_ref[...] = acc.astype(o_ref.dtype)

    part = jnp.dot(x_ref[...], w_ref[...], preferred_element_type=F32)
    if nk == 1:
        finish(part)
        return
    acc_ref, = rest
    k = pl.program_id(2)

    @pl.when(k == 0)
    def _():
        acc_ref[...] = jnp.zeros(acc_ref.shape, F32)

    acc_ref[...] += part

    @pl.when(k == nk - 1)
    def _():
        finish(acc_ref[...])


def _matmul(x, w, *, tm, tn, tk, out_dtype, col_scale=None, name="matmul"):
    m, kdim = x.shape
    n = w.shape[1]
    nk = kdim // tk
    in_specs = [pl.BlockSpec((tm, tk), lambda i, j, k: (i, k)),
                pl.BlockSpec((tk, tn), lambda i, j, k: (k, j))]
    args = [x, w]
    if col_scale is not None:
        in_specs.append(pl.BlockSpec((1, tn), lambda i, j, k: (0, j)))
        args.append(col_scale)
    return pl.pallas_call(
        functools.partial(_mm_kernel, nk=nk, scaled=col_scale is not None),
        grid=(m // tm, n // tn, nk),
        in_specs=in_specs,
        out_specs=pl.BlockSpec((tm, tn), lambda i, j, k: (i, j)),
        out_shape=jax.ShapeDtypeStruct((m, n), out_dtype),
        scratch_shapes=[] if nk == 1 else [pltpu.VMEM((tm, tn), F32)],
        compiler_params=_cparams(("arbitrary", "arbitrary", "arbitrary")),
        name=name,
    )(*args)


def _compress_kernel(a_ref, w1_ref, pos_ref, b1_ref, w2_ref, b2_ref, o_ref, *, n_cmp):
    a = a_ref[0, 0, 0]
    w1 = w1_ref[0]
    half = a.shape[1]
    top = jnp.dot(a, w1[:half], preferred_element_type=F32)
    bot = jnp.dot(a, w1[half:], preferred_element_type=F32)
    rows = a.shape[0]
    pre = top + pltpu.roll(bot, rows - 1, 0)
    pos_term = jnp.dot(pos_ref[0].astype(MXU_DTYPE), w1, preferred_element_type=F32)[0:1]
    hid = jax.nn.gelu(pre + pos_term + b1_ref[0])
    out = jnp.dot(hid.astype(MXU_DTYPE), w2_ref[0], preferred_element_type=F32) + b2_ref[0]
    ridx = jax.lax.broadcasted_iota(jnp.int32, out.shape, 0)
    o_ref[0, 0, 0] = jnp.where(ridx < n_cmp, out, 0.0)


def _compress(a, w1, pos8, b1, w2, b2, n_cmp):
    b, two, g, rows, width = a.shape
    hid = w1.shape[2]
    dk = w2.shape[2]
    return pl.pallas_call(
        functools.partial(_compress_kernel, n_cmp=n_cmp),
        grid=(two, b, g),
        in_specs=[pl.BlockSpec((1, 1, 1, rows, width), lambda s, i, j: (i, s, j, 0, 0)),
                  pl.BlockSpec((1, 2 * width, hid), lambda s, i, j: (s, 0, 0)),
                  pl.BlockSpec((1, 8, 2 * width), lambda s, i, j: (s, 0, 0)),
                  pl.BlockSpec((1, 1, hid), lambda s, i, j: (s, 0, 0)),
                  pl.BlockSpec((1, hid, dk), lambda s, i, j: (s, 0, 0)),
                  pl.BlockSpec((1, 1, dk), lambda s, i, j: (s, 0, 0))],
        out_specs=pl.BlockSpec((1, 1, 1, rows, dk), lambda s, i, j: (i, s, j, 0, 0)),
        out_shape=jax.ShapeDtypeStruct((b, two, g, rows, dk), F32),
        compiler_params=_cparams(("arbitrary", "arbitrary", "arbitrary")),
        name="compress",
    )(a, w1, pos8, b1, w2, b2)


def _rel_bucket_np(dist):
    n = np.maximum(dist, 0)
    max_exact = REL_BUCKETS // 2
    nf = np.maximum(n, 1).astype(np.float32)
    large = max_exact + (np.log(nf / max_exact) / math.log(REL_MAX_DIST / max_exact)
                         * (REL_BUCKETS - max_exact)).astype(np.int32)
    large = np.minimum(large, REL_BUCKETS - 1)
    return np.where(n < max_exact, n, large).astype(np.int32)


def _bias_index_tiles():
    w = np.arange(WINDOW + Q_BLOCK)[:, None]
    i = np.arange(Q_BLOCK)[None, :]
    d_w = i - w + WINDOW
    win = np.where((d_w >= 0) & (d_w < WINDOW), _rel_bucket_np(d_w), -1)
    m = np.arange(CMP_NEAR)[:, None]
    d_c = i - CMP_STRIDE * m + (CMP_STRIDE * CMP_FRONT - CMP_BLOCK + 1)
    near = np.where(d_c >= 0, _rel_bucket_np(d_c), -1)
    return win.astype(np.int32), near.astype(np.int32)


def _bias_kernel(tab_ref, idw_ref, idc_ref, ow_ref, os_ref, oc_ref, o31_ref):
    h = pl.program_id(0)
    far = tab_ref[REL_BUCKETS - 1, h] * LOG2E

    def lut(idx, shift):
        out = jnp.full(idx.shape, NEG, F32)
        for b in range(REL_BUCKETS):
            out = jnp.where(idx == b, tab_ref[b, h] * LOG2E - shift, out)
        return out

    ow_ref[0] = lut(idw_ref[...], 0.0)
    os_ref[0] = lut(idw_ref[...], far)
    oc_ref[0] = lut(idc_ref[...], 0.0)
    o31_ref[0] = jnp.full(o31_ref.shape[1:], far, F32)


def _bias_tiles(rel_table):
    win, near = _bias_index_tiles()
    g, r = NSA_KV_GROUPS, HEADS_PER_GROUP
    full = lambda a: pl.BlockSpec(a.shape, lambda h: (0, 0))
    head = lambda rows: pl.BlockSpec((1, rows, Q_BLOCK), lambda h: (h // r, 0, h % r))
    return pl.pallas_call(
        _bias_kernel,
        grid=(NSA_HEADS,),
        in_specs=[pl.BlockSpec(memory_space=pltpu.SMEM), full(win), full(near)],
        out_specs=[head(win.shape[0]), head(win.shape[0]), head(CMP_NEAR), head(8)],
        out_shape=[jax.ShapeDtypeStruct((g, win.shape[0], QL), F32), jax.ShapeDtypeStruct((g, win.shape[0], QL), F32),
                   jax.ShapeDtypeStruct((g, CMP_NEAR, QL), F32), jax.ShapeDtypeStruct((g, 8, QL), F32)],
        compiler_params=_cparams(("arbitrary",)),
        name="bias_tiles",
    )(rel_table, jnp.asarray(win), jnp.asarray(near))


def _bias_variants(bwin_full, n_tiles):
    g = bwin_full.shape[0]
    n_win = WINDOW // Q_BLOCK
    tiles = bwin_full.reshape(g, n_win + 1, Q_BLOCK, QL)
    masked = jnp.full((g, Q_BLOCK, QL), NEG, F32)
    out = []
    for v in range(n_tiles):
        chunk = [tiles[:, n_win - v + u] if v - u >= 0 else masked for u in range(n_tiles)]
        out.append(jnp.concatenate(chunk, axis=1))
    return jnp.stack(out, axis=1)


def _selection_map_t(n_cmp, n_slc, ncp):
    ratio = SLC_BLOCK // CMP_STRIDE
    span = CMP_BLOCK // CMP_STRIDE
    d = np.arange(n_cmp)[:, None] - ratio * np.arange(n_slc)[None, :]
    a = d[..., None] + np.arange(span)
    sel = np.sum((a >= 0) & (a < ratio), axis=-1).astype(np.float32)
    out = np.zeros((n_slc, ncp), np.float32)
    out[:, CMP_FRONT:CMP_FRONT + n_cmp] = sel.T
    return out


def _nsa_kernel(q_ref, kc_ref, vc_ref, selt_ref, ks_ref, vs_ref, kw_ref, vw_ref, g_ref,
                bnear_ref, bwin_ref, bc_ref, c31_ref, onehot_ref, o_ref,
                p_scr, selb_scr, selbf_scr, m_scr, l_scr, acc_scr, *, top_k):
    qb = pl.program_id(2)
    qn = q_ref[0]
    q = jnp.concatenate([qn[:, r * HEAD_DIM:(r + 1) * HEAD_DIM].astype(F32).T for r in range(HEADS_PER_GROUP)],
                        axis=1).astype(MXU_DTYPE)
    ncp = kc_ref.shape[3]
    n_slc = selt_ref.shape[0]
    c31 = c31_ref[0][0:1, :]
    lane_i = jax.lax.broadcasted_iota(jnp.int32, (1, QL), 1) & (Q_BLOCK - 1)
    t0 = qb * Q_BLOCK

    def vt_dot(v_rows, p):
        return jax.lax.dot_general(v_rows, p.astype(MXU_DTYPE), (((0,), (0,)), ((), ())),
                                   preferred_element_type=F32)

    def scores_softmax(s, v_rows):
        m_c = jnp.max(s, axis=0, keepdims=True)
        pe = jnp.exp2(s - m_c)
        return m_c, jnp.sum(pe, axis=0, keepdims=True), vt_dot(v_rows, pe)

    def keys(ref, tile, n_tiles):
        return ref[0, pl.ds(pl.multiple_of(tile * Q_BLOCK, Q_BLOCK), n_tiles * Q_BLOCK), :]

    def block_mask(scr, tile, n_tiles):
        blocks_per_tile = Q_BLOCK // SLC_BLOCK
        rows = [jnp.broadcast_to(scr[pl.ds(blocks_per_tile * tile + u, 1), :], (SLC_BLOCK, QL))
                for u in range(blocks_per_tile * n_tiles)]
        return jnp.concatenate(rows, axis=0)

    n_win = WINDOW // Q_BLOCK
    win_tile = jnp.maximum(qb - n_win, 0)
    near_tile = jnp.maximum(qb - 1, 0)
    s_all = jnp.dot(kc_ref[0, 0, 0].astype(MXU_DTYPE), q, preferred_element_type=F32)
    s_win = jnp.dot(keys(kw_ref, win_tile, n_win + 1), q, preferred_element_type=F32) + bwin_ref[0, 0]
    s_pair = jnp.dot(keys(ks_ref, near_tile, 2), q, preferred_element_type=F32) + bnear_ref[0, 0]

    prow = jax.lax.broadcasted_iota(jnp.int32, (ncp, QL), 0)
    near0 = pl.multiple_of(qb * (Q_BLOCK // CMP_STRIDE), 8)
    s_far = jnp.where(prow < near0, jnp.where(prow >= CMP_FRONT, s_all + c31, NEG), NEG)
    kn = kc_ref[0, 0, 0, pl.ds(near0, CMP_NEAR), :]
    mrow = jax.lax.broadcasted_iota(jnp.int32, (CMP_NEAR, QL), 0)
    s_near = jnp.dot(kn.astype(MXU_DTYPE), q, preferred_element_type=F32) + bc_ref[0]
    s_near = jnp.where(mrow >= CMP_FRONT - near0, s_near, NEG)
    mx = jnp.maximum(jnp.max(s_far, axis=0, keepdims=True), jnp.max(s_near, axis=0, keepdims=True))
    e_far = jnp.exp2(s_far - mx)
    e_near = jnp.exp2(s_near - mx)
    den = jnp.sum(e_far, axis=0, keepdims=True) + jnp.sum(e_near, axis=0, keepdims=True)
    inv = jnp.where(t0 + lane_i >= CMP_BLOCK - 1, 1.0 / den, 0.0)
    p_scr[...] = e_far * inv
    p_scr[pl.ds(near0, CMP_NEAR), :] = e_near * inv
    p = p_scr[...]
    o_c = vt_dot(vc_ref[0, 0, 0].astype(MXU_DTYPE), p)

    psum = p[:, 0:Q_BLOCK]
    for r in range(1, HEADS_PER_GROUP):
        psum = psum + p[:, r * Q_BLOCK:(r + 1) * Q_BLOCK]
    p_hi = psum.astype(MXU_DTYPE)
    p_lo = (psum - p_hi.astype(F32)).astype(MXU_DTYPE)
    selt = selt_ref[...]
    imp = (jnp.dot(selt, p_hi, preferred_element_type=F32)
           + jnp.dot(selt, p_lo, preferred_element_type=F32))

    _, l_w, acc_w = scores_softmax(s_win, keys(vw_ref, win_tile, n_win + 1))
    o_w = acc_w * (1.0 / l_w)

    jidx = jax.lax.broadcasted_iota(jnp.int32, (n_slc, Q_BLOCK), 0).astype(F32)
    iidx = jax.lax.broadcasted_iota(jnp.int32, (n_slc, Q_BLOCK), 1)
    cur = (qb * (Q_BLOCK // SLC_BLOCK) + jax.lax.shift_right_logical(iidx, int(math.log2(SLC_BLOCK)))).astype(F32)
    vis = jidx <= cur
    forced = jnp.where(jidx == 0.0, 1.0, jnp.where(jidx == cur, 1.0, jnp.where(jidx == cur - 1.0, 1.0, 0.0)))
    score = jnp.where(vis, jnp.where(forced > 0.5, FORCE, imp), -1.0)
    rem = score
    for _ in range(top_k):
        best = jnp.max(rem, axis=0, keepdims=True)
        first = jnp.min(jnp.where(rem == best, jidx, float(n_slc)), axis=0, keepdims=True)
        rem = jnp.where(jidx == first, -2.0, rem)
    n_far = jnp.maximum(qb - 1, 0)
    far_block = jidx < (n_far * (Q_BLOCK // SLC_BLOCK)).astype(F32)
    selb = jnp.where(vis, jnp.where(rem < -1.5, 0.0, NEG), NEG)
    selbf = jnp.where(far_block, selb, NEG)
    selb_scr[...] = jnp.concatenate([selb] * HEADS_PER_GROUP, axis=1)
    selbf_scr[...] = jnp.concatenate([selbf] * HEADS_PER_GROUP, axis=1)

    m0, l0, acc0 = scores_softmax(s_pair + block_mask(selb_scr, near_tile, 2), keys(vs_ref, near_tile, 2))
    m_scr[...] = m0
    l_scr[...] = l0
    acc_scr[...] = acc0

    far_blocks = FAR_TILES * Q_BLOCK // SLC_BLOCK
    pad_rows = jnp.zeros((HEAD_DIM - far_blocks, QL), MXU_DTYPE)

    def far_chunk(c, carry):
        mask_rows = selbf_scr[pl.ds(pl.multiple_of(c * far_blocks, far_blocks), far_blocks), :]
        q_aug = jnp.concatenate([q, mask_rows.astype(MXU_DTYPE), pad_rows], axis=0)
        part = FAR_TILES // FAR_PARTS
        scores = []
        for u in range(FAR_PARTS):
            k_aug = jnp.concatenate([keys(ks_ref, c * FAR_TILES + u * part, part),
                                     onehot_ref[u * part * Q_BLOCK:(u + 1) * part * Q_BLOCK, :]], axis=1)
            scores.append(jnp.dot(k_aug, q_aug, preferred_element_type=F32))
        stats = [scores_softmax(scores[u], keys(vs_ref, c * FAR_TILES + u * part, part)) for u in range(FAR_PARTS)]
        m_old = m_scr[...]
        m_new = m_old
        for m_c, _, _ in stats:
            m_new = jnp.maximum(m_new, m_c)
        a_old = jnp.exp2(m_old - m_new)
        l_new = a_old * l_scr[...]
        acc_new = a_old * acc_scr[...]
        for m_c, l_c, acc_c in stats:
            a_c = jnp.exp2(m_c - m_new)
            l_new = l_new + a_c * l_c
            acc_new = acc_new + a_c * acc_c
        l_scr[...] = l_new
        acc_scr[...] = acc_new
        m_scr[...] = m_new
        return carry

    jax.lax.fori_loop(0, (n_far + FAR_TILES - 1) // FAR_TILES, far_chunk, 0)
    o_s = acc_scr[...] * (1.0 / l_scr[...])

    gates = jax.nn.sigmoid(g_ref[0, 0, 0])
    o = gates[0:1, :] * o_c + gates[1:2, :] * o_s + gates[2:3, :] * o_w
    for r in range(HEADS_PER_GROUP):
        o_ref[0, :, r * HEAD_DIM:(r + 1) * HEAD_DIM] = o[:, r * Q_BLOCK:(r + 1) * Q_BLOCK].T.astype(o_ref.dtype)


def _nsa(att, cmp_pad, selt, kv_col, gt, bnear, bwin, bc, c31):
    b, seq, _ = att.shape
    g = cmp_pad.shape[2]
    nq = seq // Q_BLOCK
    ncp = cmp_pad.shape[3]
    n_slc = selt.shape[0]
    kv_block = lambda which: pl.BlockSpec((1, seq, HEAD_DIM), lambda i, j, k: (i, 0, kv_col + which * g + j))
    cmp_block = lambda which: pl.BlockSpec((1, 1, 1, ncp, HEAD_DIM), lambda i, j, k: (i, which, j, 0, 0))
    top_k = min(SLC_TOPK, n_slc)
    assert nq % FAR_TILES == 0 and nq > WINDOW // Q_BLOCK
    per_g = lambda rows: pl.BlockSpec((1, rows, QL), lambda i, j, k: (j, 0, 0))
    block_of_row = np.arange(FAR_TILES * Q_BLOCK)[:, None] // SLC_BLOCK
    onehot = jnp.asarray(block_of_row == np.arange(HEAD_DIM)[None, :], MXU_DTYPE)
    variant = lambda a: pl.BlockSpec((1, 1) + a.shape[2:], lambda i, j, k: (j, jnp.minimum(k, a.shape[1] - 1), 0, 0))
    return pl.pallas_call(
        functools.partial(_nsa_kernel, top_k=top_k),
        grid=(b, g, nq),
        in_specs=[pl.BlockSpec((1, Q_BLOCK, GROUP_WIDTH), lambda i, j, k: (i, k, j)),
                  cmp_block(0), cmp_block(1),
                  pl.BlockSpec((n_slc, ncp), lambda i, j, k: (0, 0)),
                  kv_block(0), kv_block(1), kv_block(2), kv_block(3),
                  pl.BlockSpec((1, 1, 1, 8, QL), lambda i, j, k: (i, j, k, 0, 0)),
                  variant(bnear), variant(bwin), per_g(CMP_NEAR), per_g(8),
                  pl.BlockSpec(onehot.shape, lambda i, j, k: (0, 0))],
        out_specs=pl.BlockSpec((1, Q_BLOCK, GROUP_WIDTH), lambda i, j, k: (i, k, j)),
        out_shape=jax.ShapeDtypeStruct((b, seq, g * GROUP_WIDTH), MXU_DTYPE),
        scratch_shapes=[pltpu.VMEM((ncp, QL), F32), pltpu.VMEM((n_slc, QL), F32), pltpu.VMEM((n_slc, QL), F32),
                        pltpu.VMEM((1, QL), F32), pltpu.VMEM((1, QL), F32), pltpu.VMEM((HEAD_DIM, QL), F32)],
        compiler_params=_cparams(("arbitrary", "arbitrary", "arbitrary")),
        name="nsa",
    )(att, cmp_pad, cmp_pad, selt, att, att, att, att, gt, bnear, bwin, bc, c31, onehot)


def _sgu_kernel(zu_ref, zv_ref, lg_ref, lb_ref, w_ref, bs_ref, o_ref):
    u = jax.nn.gelu(zu_ref[0].astype(F32))
    v = jax.nn.gelu(zv_ref[0].astype(F32))
    mu = jnp.mean(v, axis=-1, keepdims=True)
    var = jnp.mean(jnp.square(v - mu), axis=-1, keepdims=True)
    vn = ((v - mu) * jax.lax.rsqrt(var + EPS) * lg_ref[...] + lb_ref[...]).astype(MXU_DTYPE)
    tc = w_ref.shape[1]
    causal = (jax.lax.broadcasted_iota(jnp.int32, (tc, tc), 0)
              >= jax.lax.broadcasted_iota(jnp.int32, (tc, tc), 1))
    bs = bs_ref[...]
    for h in range(w_ref.shape[0]):
        sl = slice(h * SGU_GROUP_DIM, (h + 1) * SGU_GROUP_DIM)
        w = jnp.where(causal, w_ref[h], 0.0).astype(MXU_DTYPE)
        mixed = jnp.dot(w, vn[:, sl], preferred_element_type=F32) + bs[:, h:h + 1]
        o_ref[0, :, sl] = (u[:, sl] * mixed).astype(o_ref.dtype)


def _sgu(proj3, ln_g, ln_b, w_s, b_st):
    b, t, _ = proj3.shape
    gs, tc, _ = w_s.shape
    vec = pl.BlockSpec((1, SGU_WIDTH), lambda i, j: (0, 0))
    return pl.pallas_call(
        _sgu_kernel,
        grid=(b, t // tc),
        in_specs=[pl.BlockSpec((1, tc, SGU_WIDTH), lambda i, j: (i, j, 0)),
                  pl.BlockSpec((1, tc, SGU_WIDTH), lambda i, j: (i, j, 1)),
                  vec, vec,
                  pl.BlockSpec((gs, tc, tc), lambda i, j: (0, 0, 0)),
                  pl.BlockSpec((tc, gs), lambda i, j: (0, 0))],
        out_specs=pl.BlockSpec((1, tc, SGU_WIDTH), lambda i, j: (i, j, 0)),
        out_shape=jax.ShapeDtypeStruct((b, t, SGU_WIDTH), MXU_DTYPE),
        compiler_params=_cparams(("arbitrary", "arbitrary")),
        name="sgu",
    )(proj3, proj3, ln_g, ln_b, w_s, b_st)


def _merge_kernel(ya_ref, yb_ref, wa_ref, wb_ref, ga_ref, gb_ref, o_ref):
    pa = jnp.dot(ya_ref[...], wa_ref[...], preferred_element_type=F32)
    pb = jnp.dot(yb_ref[...], wb_ref[...], preferred_element_type=F32)
    merged = (jax.nn.sigmoid(ga_ref[...].astype(F32)) * pa + jax.nn.sigmoid(gb_ref[...].astype(F32)) * pb)
    o_ref[...] = merged.astype(o_ref.dtype)


def _merge(ya, yb, wa, wb, proj, ga_col, gb_col, *, tm=1024, tn=512):
    m, ka = ya.shape
    kb = yb.shape[1]
    n = wa.shape[1]
    return pl.pallas_call(
        _merge_kernel,
        grid=(m // tm, n // tn),
        in_specs=[pl.BlockSpec((tm, ka), lambda i, j: (i, 0)),
                  pl.BlockSpec((tm, kb), lambda i, j: (i, 0)),
                  pl.BlockSpec((ka, tn), lambda i, j: (0, j)),
                  pl.BlockSpec((kb, tn), lambda i, j: (0, j)),
                  pl.BlockSpec((tm, tn), lambda i, j: (i, ga_col + j)),
                  pl.BlockSpec((tm, tn), lambda i, j: (i, gb_col + j))],
        out_specs=pl.BlockSpec((tm, tn), lambda i, j: (i, j)),
        out_shape=jax.ShapeDtypeStruct((m, n), MXU_DTYPE),
        compiler_params=_cparams(("arbitrary", "arbitrary")),
        name="merge",
    )(ya, yb, wa, wb, proj, proj)


def _ffn_up_kernel(h_ref, halo_ref, wg_ref, wu_ref, cwg_ref, cwu_ref, cbg_ref, cbu_ref, o_ref, *, seq, sub):
    i = pl.program_id(0)
    tm = h_ref.shape[0]
    wg = wg_ref[...]
    wu = wu_ref[...]
    row = jax.lax.broadcasted_iota(jnp.int32, (sub, wg.shape[1]), 0)

    def conv(a, before, cw_ref, cb_ref):
        a1 = jnp.where(row == 0, before[7:8, :], pltpu.roll(a, 1, 0))
        a2 = jnp.where(row == 0, before[6:7, :], jnp.where(row == 1, before[7:8, :], pltpu.roll(a, 2, 0)))
        return cb_ref[0:1, :] + a2 * cw_ref[0:1, :] + a1 * cw_ref[1:2, :] + a * cw_ref[2:3, :]

    keep = jnp.where((i * tm) % seq == 0, 0.0, 1.0)
    before_g = jnp.dot(halo_ref[...], wg, preferred_element_type=F32) * keep
    before_u = jnp.dot(halo_ref[...], wu, preferred_element_type=F32) * keep
    for s in range(tm // sub):
        hs = h_ref[s * sub:(s + 1) * sub, :]
        ag = jnp.dot(hs, wg, preferred_element_type=F32)
        au = jnp.dot(hs, wu, preferred_element_type=F32)
        gate = conv(ag, before_g, cwg_ref, cbg_ref)
        up = conv(au, before_u, cwu_ref, cbu_ref)
        o_ref[s * sub:(s + 1) * sub, :] = (gate * jax.nn.sigmoid(gate) * up).astype(o_ref.dtype)
        before_g = ag[sub - 8:, :]
        before_u = au[sub - 8:, :]


def _ffn_up(h, w, cw8, cb8, *, seq, tm=2048, sub=512):
    m, d = h.shape
    f = w.shape[1] // 2
    tn = FFN_HALF_TILE
    nj = f // tn
    halo_blocks = tm // 8
    gate_cols = lambda rows: pl.BlockSpec((rows, tn), lambda i, j: (0, j))
    up_cols = lambda rows: pl.BlockSpec((rows, tn), lambda i, j: (0, nj + j))
    return pl.pallas_call(
        functools.partial(_ffn_up_kernel, seq=seq, sub=sub),
        grid=(m // tm, nj),
        in_specs=[pl.BlockSpec((tm, d), lambda i, j: (i, 0)),
                  pl.BlockSpec((8, d), lambda i, j: (jnp.maximum(i * halo_blocks - 1, 0), 0)),
                  gate_cols(d), up_cols(d), gate_cols(8), up_cols(8), gate_cols(8), up_cols(8)],
        out_specs=pl.BlockSpec((tm, tn), lambda i, j: (i, j)),
        out_shape=jax.ShapeDtypeStruct((m, f), MXU_DTYPE),
        compiler_params=_cparams(("arbitrary", "arbitrary")),
        name="ffn_up",
    )(h, h, w, w, cw8, cw8, cb8, cb8)


def _pad_rows(a, rows):
    return jnp.pad(a, ((0, rows - a.shape[0]), (0, 0)))


def _mixer_branches(h, w_in, cmp_pos, cmp_w1, cmp_b1, cmp_w2, cmp_b2, rel_table,
                    sgu_ln_g, sgu_ln_b, sgu_w, sgu_b):
    b, t, d = h.shape
    g, r, dk = NSA_KV_GROUPS, HEADS_PER_GROUP, HEAD_DIM
    m = b * t
    nq = t // Q_BLOCK
    n_cmp = (t - CMP_BLOCK) // CMP_STRIDE + 1
    n_slc = t // SLC_BLOCK
    rows16 = t // CMP_STRIDE
    ncp = rows16 + LANES

    o_gn = NSA_WIDTH + 6 * KV_WIDTH
    o_z = o_gn + NSA_HEADS * 3
    w_att = w_in[:, :o_gn].astype(MXU_DTYPE)
    w_gn = jnp.pad(w_in[:, o_gn:o_z], ((0, 0), (0, LANES - NSA_HEADS * 3))).astype(MXU_DTYPE)
    w_rest = w_in[:, o_z:].astype(MXU_DTYPE)
    c_ga, c_gb = 2 * SGU_WIDTH, 2 * SGU_WIDTH + d
    c_kv = [NSA_WIDTH + i * KV_WIDTH for i in range(6)]
    col_scale = jnp.ones((1, o_gn), F32).at[:, :NSA_WIDTH].set(HEAD_DIM ** -0.5 * LOG2E)

    h2d = h.reshape(m, d)
    proj = _matmul(h2d, w_rest, tm=1024, tn=1024, tk=d, out_dtype=MXU_DTYPE, name="proj_rest")
    patt = _matmul(h2d, w_att, tm=1024, tn=1024, tk=d, out_dtype=MXU_DTYPE, col_scale=col_scale, name="proj_att")
    gn = _matmul(h2d, w_gn, tm=1024, tn=LANES, tk=d, out_dtype=F32, name="proj_gates")
    proj3 = proj.reshape(b, t, proj.shape[1])
    patt3 = patt.reshape(b, t, o_gn)

    kv_c = patt3[:, :, c_kv[0]:c_kv[2]].reshape(b, rows16, CMP_STRIDE, 2, g, dk)
    a_cmp = kv_c.transpose(0, 3, 4, 1, 2, 5).reshape(b, 2, g, rows16, CMP_STRIDE * dk)
    pos8 = jnp.broadcast_to(cmp_pos.reshape(2, 1, CMP_BLOCK * dk), (2, 8, CMP_BLOCK * dk))
    cmp_out = _compress(a_cmp, cmp_w1.astype(MXU_DTYPE), pos8, cmp_b1.reshape(2, 1, CMP_HIDDEN),
                        cmp_w2.astype(MXU_DTYPE), cmp_b2.reshape(2, 1, dk), n_cmp)
    cmp_pad = jnp.pad(cmp_out, ((0, 0), (0, 0), (0, 0), (CMP_FRONT, ncp - CMP_FRONT - rows16), (0, 0)))

    gt = gn[:, :NSA_HEADS * 3].reshape(b, nq, Q_BLOCK, g, r, 3).transpose(0, 3, 1, 5, 4, 2).reshape(b, g, nq, 3, QL)
    gt = jnp.pad(gt, ((0, 0), (0, 0), (0, 0), (0, 5), (0, 0)))
    bwin_full, bwin_shifted, bc, c31 = _bias_tiles(rel_table)
    bwin = _bias_variants(bwin_full, WINDOW // Q_BLOCK + 1)
    bnear = _bias_variants(bwin_shifted, 2)
    selt = jnp.asarray(_selection_map_t(n_cmp, n_slc, ncp), MXU_DTYPE)
    y_a = _nsa(patt3, cmp_pad, selt, c_kv[2] // dk, gt, bnear, bwin, bc, c31)

    y_b = _sgu(proj3, sgu_ln_g.reshape(1, SGU_WIDTH), sgu_ln_b.reshape(1, SGU_WIDTH), sgu_w, sgu_b.T)
    return proj, y_a, y_b, c_ga, c_gb


def _token_mixer(h, w_in, cmp_pos, cmp_w1, cmp_b1, cmp_w2, cmp_b2, rel_table,
                 sgu_ln_g, sgu_ln_b, sgu_w, sgu_b, w_proj_nsa, w_proj_sgu):
    proj, y_a, y_b, c_ga, c_gb = _mixer_branches(h, w_in, cmp_pos, cmp_w1, cmp_b1, cmp_w2, cmp_b2, rel_table,
                                                 sgu_ln_g, sgu_ln_b, sgu_w, sgu_b)
    m = proj.shape[0]
    tn = 512
    return _merge(y_a.reshape(m, NSA_WIDTH), y_b.reshape(m, SGU_WIDTH),
                  w_proj_nsa.astype(MXU_DTYPE), w_proj_sgu.astype(MXU_DTYPE),
                  proj, c_ga // tn, c_gb // tn, tm=1024, tn=tn)


def kernel(x, c, w_mod, b_mod, g_norms, w_in, cmp_pos, cmp_w1, cmp_b1, cmp_w2, cmp_b2, rel_table, sgu_ln_g, sgu_ln_b, sgu_w, sgu_b, w_proj_nsa, w_proj_sgu, w_out, w_ffn_up, ffn_conv_w, ffn_conv_b, w_ffn_down):
    b, t, d = x.shape
    m = b * t
    depth = w_mod.shape[0]
    c8 = jnp.pad(c, ((0, 8 - b), (0, 0)))
    for l in range(depth):
        mod = _mod(c8, w_mod[l], b_mod[l].reshape(1, -1))[:b]
        sh1, sc1, gt1, sh2, sc2, gt2 = [v.reshape(b, 1, d) for v in jnp.split(mod, 6, axis=-1)]
        gn = g_norms[l]

        h = _norm_mod(x, gn[0:1], sh1, sc1)
        merged = _token_mixer(h, w_in[l], cmp_pos[l], cmp_w1[l], cmp_b1[l], cmp_w2[l], cmp_b2[l], rel_table,
                              sgu_ln_g[l], sgu_ln_b[l], sgu_w[l], sgu_b[l], w_proj_nsa[l], w_proj_sgu[l])
        y = _matmul(merged, w_out[l].astype(MXU_DTYPE), tm=1024, tn=1024, tk=d, out_dtype=MXU_DTYPE, name="out_proj")
        x1, h2 = _resid_norm_mod(x, y.reshape(b, t, d), gn[1:2], gt1, gn[2:3], sh2, sc2)

        cw8 = _pad_rows(ffn_conv_w[l], 8)
        cb8 = _pad_rows(ffn_conv_b[l].reshape(1, -1), 8)
        act = _ffn_up(h2.reshape(m, d), w_ffn_up[l].astype(MXU_DTYPE), cw8, cb8, seq=t)
        d_ff = act.shape[1]
        y2 = _matmul(act, w_ffn_down[l].astype(MXU_DTYPE), tm=512, tn=512, tk=d_ff, out_dtype=MXU_DTYPE,
                     name="ffn_down")
        x = _resid_norm(x1, y2.reshape(b, t, d), gn[3:4], gt2)
    return x
```

```python
import functools
import math

import jax
import jax.numpy as jnp
import numpy as np
from jax.experimental import pallas as pl
from jax.experimental.pallas import tpu as pltpu

NSA_HEADS = 16
NSA_KV_GROUPS = 4
HEADS_PER_GROUP = NSA_HEADS // NSA_KV_GROUPS
HEAD_DIM = 128
CMP_BLOCK = 32
CMP_STRIDE = 16
CMP_HIDDEN = 256
SLC_BLOCK = 64
SLC_TOPK = 16
WINDOW = 512
Q_BLOCK = 256
SGU_GROUPS = 16
SGU_GROUP_DIM = 128
SGU_CHUNK = 128
SGU_WIDTH = SGU_GROUPS * SGU_GROUP_DIM
REL_BUCKETS = 32
REL_MAX_DIST = 128
CONV_WIDTH = 3
EPS = 1e-6
NEG = -1e30
FORCE = 1e6
LOG2E = math.log2(math.e)

NSA_WIDTH = NSA_HEADS * HEAD_DIM
KV_WIDTH = NSA_KV_GROUPS * HEAD_DIM
LANES = 128
QL = HEADS_PER_GROUP * Q_BLOCK
GROUP_WIDTH = HEADS_PER_GROUP * HEAD_DIM
CMP_FRONT = 16
CMP_NEAR = 32
FFN_HALF_TILE = 256
FAR_TILES = 4
FAR_PARTS = 2

MXU_DTYPE = jnp.bfloat16
F32 = jnp.float32
VMEM_LIMIT = 56 * 1024 * 1024


def _cparams(sem):
    return pltpu.CompilerParams(dimension_semantics=sem, vmem_limit_bytes=VMEM_LIMIT)


def _mod_kernel(c_ref, w_ref, b_ref, o_ref):
    c = c_ref[...]
    ca = c * jax.nn.sigmoid(c)
    o_ref[...] = jnp.dot(ca.astype(MXU_DTYPE), w_ref[...].astype(MXU_DTYPE),
                         preferred_element_type=F32) + b_ref[...]


def _mod(c8, w_mod, b_mod, tn=512):
    rows, d = c8.shape
    n = w_mod.shape[1]
    return pl.pallas_call(
        _mod_kernel,
        grid=(n // tn,),
        in_specs=[pl.BlockSpec((rows, d), lambda j: (0, 0)),
                  pl.BlockSpec((d, tn), lambda j: (0, j)),
                  pl.BlockSpec((1, tn), lambda j: (0, j))],
        out_specs=pl.BlockSpec((rows, tn), lambda j: (0, j)),
        out_shape=jax.ShapeDtypeStruct((rows, n), F32),
        compiler_params=_cparams(("arbitrary",)),
        name="mod",
    )(c8, w_mod, b_mod)


def _rms(x, g):
    return x * jax.lax.rsqrt(jnp.mean(x * x, axis=-1, keepdims=True) + EPS) * g


def _norm_mod_kernel(x_ref, g_ref, sh_ref, sc_ref, o_ref):
    y = _rms(x_ref[0], g_ref[...])
    o_ref[0] = (y * (1.0 + sc_ref[0]) + sh_ref[0]).astype(o_ref.dtype)


def _norm_mod(x, g, sh, sc, tm=256):
    b, t, d = x.shape
    row = pl.BlockSpec((1, 1, d), lambda i, j: (i, 0, 0))
    return pl.pallas_call(
        _norm_mod_kernel,
        grid=(b, t // tm),
        in_specs=[pl.BlockSpec((1, tm, d), lambda i, j: (i, j, 0)),
                  pl.BlockSpec((1, d), lambda i, j: (0, 0)), row, row],
        out_specs=pl.BlockSpec((1, tm, d), lambda i, j: (i, j, 0)),
        out_shape=jax.ShapeDtypeStruct((b, t, d), MXU_DTYPE),
        compiler_params=_cparams(("arbitrary", "arbitrary")),
        name="norm_mod",
    )(x, g, sh, sc)


def _resid_norm_mod_kernel(x_ref, y_ref, g1_ref, gt_ref, g2_ref, sh_ref, sc_ref, x1_ref, h_ref):
    x1 = x_ref[0] + gt_ref[0] * _rms(y_ref[0].astype(F32), g1_ref[...])
    x1_ref[0] = x1
    h_ref[0] = (_rms(x1, g2_ref[...]) * (1.0 + sc_ref[0]) + sh_ref[0]).astype(h_ref.dtype)


def _resid_norm_mod(x, y, g1, gt, g2, sh, sc, tm=256):
    b, t, d = x.shape
    row = pl.BlockSpec((1, 1, d), lambda i, j: (i, 0, 0))
    gspec = pl.BlockSpec((1, d), lambda i, j: (0, 0))
    blk = pl.BlockSpec((1, tm, d), lambda i, j: (i, j, 0))
    return pl.pallas_call(
        _resid_norm_mod_kernel,
        grid=(b, t // tm),
        in_specs=[blk, blk, gspec, row, gspec, row, row],
        out_specs=[blk, blk],
        out_shape=[jax.ShapeDtypeStruct((b, t, d), F32), jax.ShapeDtypeStruct((b, t, d), MXU_DTYPE)],
        compiler_params=_cparams(("arbitrary", "arbitrary")),
        name="resid_norm_mod",
    )(x, y, g1, gt, g2, sh, sc)


def _resid_norm_kernel(x_ref, y_ref, g_ref, gt_ref, o_ref):
    o_ref[0] = x_ref[0] + gt_ref[0] * _rms(y_ref[0].astype(F32), g_ref[...])


def _resid_norm(x, y, g, gt, tm=256):
    b, t, d = x.shape
    blk = pl.BlockSpec((1, tm, d), lambda i, j: (i, j, 0))
    return pl.pallas_call(
        _resid_norm_kernel,
        grid=(b, t // tm),
        in_specs=[blk, blk, pl.BlockSpec((1, d), lambda i, j: (0, 0)),
                  pl.BlockSpec((1, 1, d), lambda i, j: (i, 0, 0))],
        out_specs=blk,
        out_shape=jax.ShapeDtypeStruct((b, t, d), F32),
        compiler_params=_cparams(("arbitrary", "arbitrary")),
        name="resid_norm",
    )(x, y, g, gt)


def _mm_kernel(x_ref, w_ref, *rest):
    acc = jnp.dot(x_ref[...], w_ref[...], preferred_element_type=F32)
    if len(rest) == 2:
        acc = acc * rest[0][...]
    o_ref = rest[-1]
    o_ref[...] = acc.astype(o_ref.dtype)


def _matmul(x, w, *, tm, tn, out_dtype, col_scale=None, name="matmul"):
    m, kdim = x.shape
    n = w.shape[1]
    in_specs = [pl.BlockSpec((tm, kdim), lambda i, j: (i, 0)),
                pl.BlockSpec((kdim, tn), lambda i, j: (0, j))]
    args = [x, w]
    if col_scale is not None:
        in_specs.append(pl.BlockSpec((1, tn), lambda i, j: (0, j)))
        args.append(col_scale)
    return pl.pallas_call(
        _mm_kernel,
        grid=(m // tm, n // tn),
        in_specs=in_specs,
        out_specs=pl.BlockSpec((tm, tn), lambda i, j: (i, j)),
        out_shape=jax.ShapeDtypeStruct((m, n), out_dtype),
        compiler_params=_cparams(("arbitrary", "arbitrary")),
        name=name,
    )(*args)


def _compress_kernel(a_ref, w1_ref, pos_ref, b1_ref, w2_ref, b2_ref, o_ref, *, n_cmp):
    a = a_ref[0, 0, 0]
    w1 = w1_ref[0]
    half = a.shape[1]
    top = jnp.dot(a, w1[:half], preferred_element_type=F32)
    bot = jnp.dot(a, w1[half:], preferred_element_type=F32)
    rows = a.shape[0]
    pre = top + pltpu.roll(bot, rows - 1, 0)
    pos_term = jnp.dot(pos_ref[0].astype(MXU_DTYPE), w1, preferred_element_type=F32)[0:1]
    hid = jax.nn.gelu(pre + pos_term + b1_ref[0])
    out = jnp.dot(hid.astype(MXU_DTYPE), w2_ref[0], preferred_element_type=F32) + b2_ref[0]
    ridx = jax.lax.broadcasted_iota(jnp.int32, out.shape, 0)
    o_ref[0, 0, 0] = jnp.where(ridx < n_cmp, out, 0.0)


def _compress(a, w1, pos8, b1, w2, b2, n_cmp):
    b, two, g, rows, width = a.shape
    hid = w1.shape[2]
    dk = w2.shape[2]
    return pl.pallas_call(
        functools.partial(_compress_kernel, n_cmp=n_cmp),
        grid=(two, b, g),
        in_specs=[pl.BlockSpec((1, 1, 1, rows, width), lambda s, i, j: (i, s, j, 0, 0)),
                  pl.BlockSpec((1, 2 * width, hid), lambda s, i, j: (s, 0, 0)),
                  pl.BlockSpec((1, 8, 2 * width), lambda s, i, j: (s, 0, 0)),
                  pl.BlockSpec((1, 1, hid), lambda s, i, j: (s, 0, 0)),
                  pl.BlockSpec((1, hid, dk), lambda s, i, j: (s, 0, 0)),
                  pl.BlockSpec((1, 1, dk), lambda s, i, j: (s, 0, 0))],
        out_specs=pl.BlockSpec((1, 1, 1, rows, dk), lambda s, i, j: (i, s, j, 0, 0)),
        out_shape=jax.ShapeDtypeStruct((b, two, g, rows, dk), F32),
        compiler_params=_cparams(("arbitrary", "arbitrary", "arbitrary")),
        name="compress",
    )(a, w1, pos8, b1, w2, b2)


def _rel_bucket_np(dist):
    n = np.maximum(dist, 0)
    max_exact = REL_BUCKETS // 2
    nf = np.maximum(n, 1).astype(np.float32)
    large = max_exact + (np.log(nf / max_exact) / math.log(REL_MAX_DIST / max_exact)
                         * (REL_BUCKETS - max_exact)).astype(np.int32)
    large = np.minimum(large, REL_BUCKETS - 1)
    return np.where(n < max_exact, n, large).astype(np.int32)


def _bias_index_tiles():
    w = np.arange(WINDOW + Q_BLOCK)[:, None]
    i = np.arange(Q_BLOCK)[None, :]
    d_w = i - w + WINDOW
    win = np.where((d_w >= 0) & (d_w < WINDOW), _rel_bucket_np(d_w), -1)
    m = np.arange(CMP_NEAR)[:, None]
    d_c = i - CMP_STRIDE * m + (CMP_STRIDE * CMP_FRONT - CMP_BLOCK + 1)
    near = np.where(d_c >= 0, _rel_bucket_np(d_c), -1)
    return win.astype(np.int32), near.astype(np.int32)


def _bias_kernel(tab_ref, idw_ref, idc_ref, ow_ref, os_ref, oc_ref, o31_ref):
    h = pl.program_id(0)
    far = tab_ref[REL_BUCKETS - 1, h] * LOG2E

    def lut(idx, shift):
        out = jnp.full(idx.shape, NEG, F32)
        for b in range(REL_BUCKETS):
            out = jnp.where(idx == b, tab_ref[b, h] * LOG2E - shift, out)
        return out

    ow_ref[0] = lut(idw_ref[...], 0.0)
    os_ref[0] = lut(idw_ref[...], far)
    oc_ref[0] = lut(idc_ref[...], 0.0)
    o31_ref[0] = jnp.full(o31_ref.shape[1:], far, F32)


def _bias_tiles(rel_table):
    win, near = _bias_index_tiles()
    g, r = NSA_KV_GROUPS, HEADS_PER_GROUP
    full = lambda a: pl.BlockSpec(a.shape, lambda h: (0, 0))
    head = lambda rows: pl.BlockSpec((1, rows, Q_BLOCK), lambda h: (h // r, 0, h % r))
    return pl.pallas_call(
        _bias_kernel,
        grid=(NSA_HEADS,),
        in_specs=[pl.BlockSpec(memory_space=pltpu.SMEM), full(win), full(near)],
        out_specs=[head(win.shape[0]), head(win.shape[0]), head(CMP_NEAR), head(8)],
        out_shape=[jax.ShapeDtypeStruct((g, win.shape[0], QL), F32), jax.ShapeDtypeStruct((g, win.shape[0], QL), F32),
                   jax.ShapeDtypeStruct((g, CMP_NEAR, QL), F32), jax.ShapeDtypeStruct((g, 8, QL), F32)],
        compiler_params=_cparams(("arbitrary",)),
        name="bias_tiles",
    )(rel_table, jnp.asarray(win), jnp.asarray(near))


def _bias_variants(bwin_full, n_tiles):
    g = bwin_full.shape[0]
    n_win = WINDOW // Q_BLOCK
    tiles = bwin_full.reshape(g, n_win + 1, Q_BLOCK, QL)
    masked = jnp.full((g, Q_BLOCK, QL), NEG, F32)
    out = []
    for v in range(n_tiles):
        chunk = [tiles[:, n_win - v + u] if v - u >= 0 else masked for u in range(n_tiles)]
        out.append(jnp.concatenate(chunk, axis=1))
    return jnp.stack(out, axis=1)


def _selection_map_t(n_cmp, n_slc, ncp):
    ratio = SLC_BLOCK // CMP_STRIDE
    span = CMP_BLOCK // CMP_STRIDE
    d = np.arange(n_cmp)[:, None] - ratio * np.arange(n_slc)[None, :]
    a = d[..., None] + np.arange(span)
    sel = np.sum((a >= 0) & (a < ratio), axis=-1).astype(np.float32)
    out = np.zeros((n_slc, ncp), np.float32)
    out[:, CMP_FRONT:CMP_FRONT + n_cmp] = sel.T
    return out


def _nsa_kernel(q_ref, kc_ref, vc_ref, selt_ref, ks_ref, vs_ref, kw_ref, vw_ref, g_ref,
                bnear_ref, bwin_ref, bc_ref, c31_ref, onehot_ref, o_ref,
                p_scr, selb_scr, selbf_scr, m_scr, l_scr, acc_scr, *, top_k):
    qb = pl.program_id(2)
    qn = q_ref[0]
    q = jnp.concatenate([qn[:, r * HEAD_DIM:(r + 1) * HEAD_DIM].astype(F32).T for r in range(HEADS_PER_GROUP)],
                        axis=1).astype(MXU_DTYPE)
    ncp = kc_ref.shape[3]
    n_slc = selt_ref.shape[0]
    c31 = c31_ref[0][0:1, :]
    lane_i = jax.lax.broadcasted_iota(jnp.int32, (1, QL), 1) & (Q_BLOCK - 1)
    t0 = qb * Q_BLOCK

    def vt_dot(v_rows, p):
        return jax.lax.dot_general(v_rows, p.astype(MXU_DTYPE), (((0,), (0,)), ((), ())),
                                   preferred_element_type=F32)

    def scores_softmax(s, v_rows):
        m_c = jnp.max(s, axis=0, keepdims=True)
        pe = jnp.exp2(s - m_c)
        return m_c, jnp.sum(pe, axis=0, keepdims=True), vt_dot(v_rows, pe)

    def keys(ref, tile, n_tiles):
        return ref[0, pl.ds(pl.multiple_of(tile * Q_BLOCK, Q_BLOCK), n_tiles * Q_BLOCK), :]

    def block_mask(scr, tile, n_tiles):
        blocks_per_tile = Q_BLOCK // SLC_BLOCK
        rows = [jnp.broadcast_to(scr[pl.ds(blocks_per_tile * tile + u, 1), :], (SLC_BLOCK, QL))
                for u in range(blocks_per_tile * n_tiles)]
        return jnp.concatenate(rows, axis=0)

    n_win = WINDOW // Q_BLOCK
    win_tile = jnp.maximum(qb - n_win, 0)
    near_tile = jnp.maximum(qb - 1, 0)
    s_all = jnp.dot(kc_ref[0, 0, 0].astype(MXU_DTYPE), q, preferred_element_type=F32)
    s_win = jnp.dot(keys(kw_ref, win_tile, n_win + 1), q, preferred_element_type=F32) + bwin_ref[0, 0]
    s_pair = jnp.dot(keys(ks_ref, near_tile, 2), q, preferred_element_type=F32) + bnear_ref[0, 0]

    prow = jax.lax.broadcasted_iota(jnp.int32, (ncp, QL), 0)
    near0 = pl.multiple_of(qb * (Q_BLOCK // CMP_STRIDE), 8)
    s_far = jnp.where(prow < near0, jnp.where(prow >= CMP_FRONT, s_all + c31, NEG), NEG)
    kn = kc_ref[0, 0, 0, pl.ds(near0, CMP_NEAR), :]
    mrow = jax.lax.broadcasted_iota(jnp.int32, (CMP_NEAR, QL), 0)
    s_near = jnp.dot(kn.astype(MXU_DTYPE), q, preferred_element_type=F32) + bc_ref[0]
    s_near = jnp.where(mrow >= CMP_FRONT - near0, s_near, NEG)
    mx = jnp.maximum(jnp.max(s_far, axis=0, keepdims=True), jnp.max(s_near, axis=0, keepdims=True))
    e_far = jnp.exp2(s_far - mx)
    e_near = jnp.exp2(s_near - mx)
    den = jnp.sum(e_far, axis=0, keepdims=True) + jnp.sum(e_near, axis=0, keepdims=True)
    inv = jnp.where(t0 + lane_i >= CMP_BLOCK - 1, 1.0 / den, 0.0)
    p_scr[...] = e_far * inv
    p_scr[pl.ds(near0, CMP_NEAR), :] = e_near * inv
    p = p_scr[...]
    o_c = vt_dot(vc_ref[0, 0, 0].astype(MXU_DTYPE), p)

    psum = p[:, 0:Q_BLOCK]
    for r in range(1, HEADS_PER_GROUP):
        psum = psum + p[:, r * Q_BLOCK:(r + 1) * Q_BLOCK]
    p_hi = psum.astype(MXU_DTYPE)
    p_lo = (psum - p_hi.astype(F32)).astype(MXU_DTYPE)
    selt = selt_ref[...]
    imp = (jnp.dot(selt, p_hi, preferred_element_type=F32)
           + jnp.dot(selt, p_lo, preferred_element_type=F32))

    _, l_w, acc_w = scores_softmax(s_win, keys(vw_ref, win_tile, n_win + 1))
    o_w = acc_w * (1.0 / l_w)

    jidx = jax.lax.broadcasted_iota(jnp.int32, (n_slc, Q_BLOCK), 0).astype(F32)
    iidx = jax.lax.broadcasted_iota(jnp.int32, (n_slc, Q_BLOCK), 1)
    cur = (qb * (Q_BLOCK // SLC_BLOCK) + jax.lax.shift_right_logical(iidx, int(math.log2(SLC_BLOCK)))).astype(F32)
    vis = jidx <= cur
    forced = jnp.where(jidx == 0.0, 1.0, jnp.where(jidx == cur, 1.0, jnp.where(jidx == cur - 1.0, 1.0, 0.0)))
    score = jnp.where(vis, jnp.where(forced > 0.5, FORCE, imp), -1.0)
    rem = score
    for _ in range(top_k):
        best = jnp.max(rem, axis=0, keepdims=True)
        first = jnp.min(jnp.where(rem == best, jidx, float(n_slc)), axis=0, keepdims=True)
        rem = jnp.where(jidx == first, -2.0, rem)
    n_far = jnp.maximum(qb - 1, 0)
    far_block = jidx < (n_far * (Q_BLOCK // SLC_BLOCK)).astype(F32)
    selb = jnp.where(vis, jnp.where(rem < -1.5, 0.0, NEG), NEG)
    selbf = jnp.where(far_block, selb, NEG)
    selb_scr[...] = jnp.concatenate([selb] * HEADS_PER_GROUP, axis=1)
    selbf_scr[...] = jnp.concatenate([selbf] * HEADS_PER_GROUP, axis=1)

    m0, l0, acc0 = scores_softmax(s_pair + block_mask(selb_scr, near_tile, 2), keys(vs_ref, near_tile, 2))
    m_scr[...] = m0
    l_scr[...] = l0
    acc_scr[...] = acc0

    far_blocks = FAR_TILES * Q_BLOCK // SLC_BLOCK
    pad_rows = jnp.zeros((HEAD_DIM - far_blocks, QL), MXU_DTYPE)

    def far_chunk(c, carry):
        mask_rows = selbf_scr[pl.ds(pl.multiple_of(c * far_blocks, far_blocks), far_blocks), :]
        q_aug = jnp.concatenate([q, mask_rows.astype(MXU_DTYPE), pad_rows], axis=0)
        part = FAR_TILES // FAR_PARTS
        scores = []
        for u in range(FAR_PARTS):
            k_aug = jnp.concatenate([keys(ks_ref, c * FAR_TILES + u * part, part),
                                     onehot_ref[u * part * Q_BLOCK:(u + 1) * part * Q_BLOCK, :]], axis=1)
            scores.append(jnp.dot(k_aug, q_aug, preferred_element_type=F32))
        stats = [scores_softmax(scores[u], keys(vs_ref, c * FAR_TILES + u * part, part)) for u in range(FAR_PARTS)]
        m_old = m_scr[...]
        m_new = m_old
        for m_c, _, _ in stats:
            m_new = jnp.maximum(m_new, m_c)
        a_old = jnp.exp2(m_old - m_new)
        l_new = a_old * l_scr[...]
        acc_new = a_old * acc_scr[...]
        for m_c, l_c, acc_c in stats:
            a_c = jnp.exp2(m_c - m_new)
            l_new = l_new + a_c * l_c
            acc_new = acc_new + a_c * acc_c
        l_scr[...] = l_new
        acc_scr[...] = acc_new
        m_scr[...] = m_new
        return carry

    jax.lax.fori_loop(0, (n_far + FAR_TILES - 1) // FAR_TILES, far_chunk, 0)
    o_s = acc_scr[...] * (1.0 / l_scr[...])

    gates = jax.nn.sigmoid(g_ref[0, 0, 0])
    o = gates[0:1, :] * o_c + gates[1:2, :] * o_s + gates[2:3, :] * o_w
    for r in range(HEADS_PER_GROUP):
        o_ref[0, :, r * HEAD_DIM:(r + 1) * HEAD_DIM] = o[:, r * Q_BLOCK:(r + 1) * Q_BLOCK].T.astype(o_ref.dtype)


def _nsa(att, cmp_pad, selt, kv_col, gt, bnear, bwin, bc, c31):
    b, seq, _ = att.shape
    g = cmp_pad.shape[2]
    nq = seq // Q_BLOCK
    ncp = cmp_pad.shape[3]
    n_slc = selt.shape[0]
    kv_block = lambda which: pl.BlockSpec((1, seq, HEAD_DIM), lambda i, j, k: (i, 0, kv_col + which * g + j))
    cmp_block = lambda which: pl.BlockSpec((1, 1, 1, ncp, HEAD_DIM), lambda i, j, k: (i, which, j, 0, 0))
    top_k = min(SLC_TOPK, n_slc)
    assert nq % FAR_TILES == 0 and nq > WINDOW // Q_BLOCK
    per_g = lambda rows: pl.BlockSpec((1, rows, QL), lambda i, j, k: (j, 0, 0))
    block_of_row = np.arange(FAR_TILES * Q_BLOCK)[:, None] // SLC_BLOCK
    onehot = jnp.asarray(block_of_row == np.arange(HEAD_DIM)[None, :], MXU_DTYPE)
    variant = lambda a: pl.BlockSpec((1, 1) + a.shape[2:], lambda i, j, k: (j, jnp.minimum(k, a.shape[1] - 1), 0, 0))
    return pl.pallas_call(
        functools.partial(_nsa_kernel, top_k=top_k),
        grid=(b, g, nq),
        in_specs=[pl.BlockSpec((1, Q_BLOCK, GROUP_WIDTH), lambda i, j, k: (i, k, j)),
                  cmp_block(0), cmp_block(1),
                  pl.BlockSpec((n_slc, ncp), lambda i, j, k: (0, 0)),
                  kv_block(0), kv_block(1), kv_block(2), kv_block(3),
                  pl.BlockSpec((1, 1, 1, 8, QL), lambda i, j, k: (i, j, k, 0, 0)),
                  variant(bnear), variant(bwin), per_g(CMP_NEAR), per_g(8),
                  pl.BlockSpec(onehot.shape, lambda i, j, k: (0, 0))],
        out_specs=pl.BlockSpec((1, Q_BLOCK, GROUP_WIDTH), lambda i, j, k: (i, k, j)),
        out_shape=jax.ShapeDtypeStruct((b, seq, g * GROUP_WIDTH), MXU_DTYPE),
        scratch_shapes=[pltpu.VMEM((ncp, QL), F32), pltpu.VMEM((n_slc, QL), F32), pltpu.VMEM((n_slc, QL), F32),
                        pltpu.VMEM((1, QL), F32), pltpu.VMEM((1, QL), F32), pltpu.VMEM((HEAD_DIM, QL), F32)],
        compiler_params=_cparams(("arbitrary", "arbitrary", "arbitrary")),
        name="nsa",
    )(att, cmp_pad, cmp_pad, selt, att, att, att, att, gt, bnear, bwin, bc, c31, onehot)


def _sgu_kernel(zu_ref, zv_ref, lg_ref, lb_ref, w_ref, bs_ref, o_ref):
    u = jax.nn.gelu(zu_ref[0].astype(F32))
    v = jax.nn.gelu(zv_ref[0].astype(F32))
    mu = jnp.mean(v, axis=-1, keepdims=True)
    var = jnp.mean(jnp.square(v - mu), axis=-1, keepdims=True)
    vn = ((v - mu) * jax.lax.rsqrt(var + EPS) * lg_ref[...] + lb_ref[...]).astype(MXU_DTYPE)
    tc = w_ref.shape[1]
    causal = (jax.lax.broadcasted_iota(jnp.int32, (tc, tc), 0)
              >= jax.lax.broadcasted_iota(jnp.int32, (tc, tc), 1))
    bs = bs_ref[...]
    for h in range(w_ref.shape[0]):
        sl = slice(h * SGU_GROUP_DIM, (h + 1) * SGU_GROUP_DIM)
        w = jnp.where(causal, w_ref[h], 0.0).astype(MXU_DTYPE)
        mixed = jnp.dot(w, vn[:, sl], preferred_element_type=F32) + bs[:, h:h + 1]
        o_ref[0, :, sl] = (u[:, sl] * mixed).astype(o_ref.dtype)


def _sgu(proj3, ln_g, ln_b, w_s, b_st):
    b, t, _ = proj3.shape
    gs, tc, _ = w_s.shape
    vec = pl.BlockSpec((1, SGU_WIDTH), lambda i, j: (0, 0))
    return pl.pallas_call(
        _sgu_kernel,
        grid=(b, t // tc),
        in_specs=[pl.BlockSpec((1, tc, SGU_WIDTH), lambda i, j: (i, j, 0)),
                  pl.BlockSpec((1, tc, SGU_WIDTH), lambda i, j: (i, j, 1)),
                  vec, vec,
                  pl.BlockSpec((gs, tc, tc), lambda i, j: (0, 0, 0)),
                  pl.BlockSpec((tc, gs), lambda i, j: (0, 0))],
        out_specs=pl.BlockSpec((1, tc, SGU_WIDTH), lambda i, j: (i, j, 0)),
        out_shape=jax.ShapeDtypeStruct((b, t, SGU_WIDTH), MXU_DTYPE),
        compiler_params=_cparams(("arbitrary", "arbitrary")),
        name="sgu",
    )(proj3, proj3, ln_g, ln_b, w_s, b_st)


def _merge_kernel(ya_ref, yb_ref, wa_ref, wb_ref, ga_ref, gb_ref, o_ref):
    pa = jnp.dot(ya_ref[...], wa_ref[...], preferred_element_type=F32)
    pb = jnp.dot(yb_ref[...], wb_ref[...], preferred_element_type=F32)
    merged = (jax.nn.sigmoid(ga_ref[...].astype(F32)) * pa + jax.nn.sigmoid(gb_ref[...].astype(F32)) * pb)
    o_ref[...] = merged.astype(o_ref.dtype)


def _merge(ya, yb, wa, wb, proj, ga_col, gb_col, *, tm=1024, tn=512):
    m, ka = ya.shape
    kb = yb.shape[1]
    n = wa.shape[1]
    return pl.pallas_call(
        _merge_kernel,
        grid=(m // tm, n // tn),
        in_specs=[pl.BlockSpec((tm, ka), lambda i, j: (i, 0)),
                  pl.BlockSpec((tm, kb), lambda i, j: (i, 0)),
                  pl.BlockSpec((ka, tn), lambda i, j: (0, j)),
                  pl.BlockSpec((kb, tn), lambda i, j: (0, j)),
                  pl.BlockSpec((tm, tn), lambda i, j: (i, ga_col + j)),
                  pl.BlockSpec((tm, tn), lambda i, j: (i, gb_col + j))],
        out_specs=pl.BlockSpec((tm, tn), lambda i, j: (i, j)),
        out_shape=jax.ShapeDtypeStruct((m, n), MXU_DTYPE),
        compiler_params=_cparams(("arbitrary", "arbitrary")),
        name="merge",
    )(ya, yb, wa, wb, proj, proj)


def _ffn_up_kernel(h_ref, halo_ref, wg_ref, wu_ref, cwg_ref, cwu_ref, cbg_ref, cbu_ref, o_ref, *, seq, sub):
    i = pl.program_id(0)
    tm = h_ref.shape[0]
    wg = wg_ref[...]
    wu = wu_ref[...]
    row = jax.lax.broadcasted_iota(jnp.int32, (sub, wg.shape[1]), 0)

    def conv(a, before, cw_ref, cb_ref):
        a1 = jnp.where(row == 0, before[7:8, :], pltpu.roll(a, 1, 0))
        a2 = jnp.where(row == 0, before[6:7, :], jnp.where(row == 1, before[7:8, :], pltpu.roll(a, 2, 0)))
        return cb_ref[0:1, :] + a2 * cw_ref[0:1, :] + a1 * cw_ref[1:2, :] + a * cw_ref[2:3, :]

    keep = jnp.where((i * tm) % seq == 0, 0.0, 1.0)
    before_g = jnp.dot(halo_ref[...], wg, preferred_element_type=F32) * keep
    before_u = jnp.dot(halo_ref[...], wu, preferred_element_type=F32) * keep
    for s in range(tm // sub):
        hs = h_ref[s * sub:(s + 1) * sub, :]
        ag = jnp.dot(hs, wg, preferred_element_type=F32)
        au = jnp.dot(hs, wu, preferred_element_type=F32)
        gate = conv(ag, before_g, cwg_ref, cbg_ref)
        up = conv(au, before_u, cwu_ref, cbu_ref)
        o_ref[s * sub:(s + 1) * sub, :] = (gate * jax.nn.sigmoid(gate) * up).astype(o_ref.dtype)
        before_g = ag[sub - 8:, :]
        before_u = au[sub - 8:, :]


def _ffn_up(h, w, cw8, cb8, *, seq, tm=2048, sub=1024):
    m, d = h.shape
    f = w.shape[1] // 2
    tn = FFN_HALF_TILE
    nj = f // tn
    halo_blocks = tm // 8
    assert seq % tm == 0 and tm % sub == 0
    gate_cols = lambda rows: pl.BlockSpec((rows, tn), lambda i, j: (0, j))
    up_cols = lambda rows: pl.BlockSpec((rows, tn), lambda i, j: (0, nj + j))
    return pl.pallas_call(
        functools.partial(_ffn_up_kernel, seq=seq, sub=sub),
        grid=(m // tm, nj),
        in_specs=[pl.BlockSpec((tm, d), lambda i, j: (i, 0)),
                  pl.BlockSpec((8, d), lambda i, j: (jnp.maximum(i * halo_blocks - 1, 0), 0)),
                  gate_cols(d), up_cols(d), gate_cols(8), up_cols(8), gate_cols(8), up_cols(8)],
        out_specs=pl.BlockSpec((tm, tn), lambda i, j: (i, j)),
        out_shape=jax.ShapeDtypeStruct((m, f), MXU_DTYPE),
        compiler_params=_cparams(("arbitrary", "arbitrary")),
        name="ffn_up",
    )(h, h, w, w, cw8, cw8, cb8, cb8)


def _pad_rows(a, rows):
    return jnp.pad(a, ((0, rows - a.shape[0]), (0, 0)))


def _mixer_branches(h, w_in, cmp_pos, cmp_w1, cmp_b1, cmp_w2, cmp_b2, rel_table,
                    sgu_ln_g, sgu_ln_b, sgu_w, sgu_b):
    b, t, d = h.shape
    g, r, dk = NSA_KV_GROUPS, HEADS_PER_GROUP, HEAD_DIM
    m = b * t
    nq = t // Q_BLOCK
    n_cmp = (t - CMP_BLOCK) // CMP_STRIDE + 1
    n_slc = t // SLC_BLOCK
    rows16 = t // CMP_STRIDE
    ncp = rows16 + LANES

    o_gn = NSA_WIDTH + 6 * KV_WIDTH
    o_z = o_gn + NSA_HEADS * 3
    w_att = w_in[:, :o_gn].astype(MXU_DTYPE)
    w_gn = jnp.pad(w_in[:, o_gn:o_z], ((0, 0), (0, LANES - NSA_HEADS * 3))).astype(MXU_DTYPE)
    w_rest = w_in[:, o_z:].astype(MXU_DTYPE)
    c_ga, c_gb = 2 * SGU_WIDTH, 2 * SGU_WIDTH + d
    c_kv = [NSA_WIDTH + i * KV_WIDTH for i in range(6)]
    col_scale = jnp.ones((1, o_gn), F32).at[:, :NSA_WIDTH].set(HEAD_DIM ** -0.5 * LOG2E)

    h2d = h.reshape(m, d)
    proj = _matmul(h2d, w_rest, tm=1024, tn=1024, out_dtype=MXU_DTYPE, name="proj_rest")
    patt = _matmul(h2d, w_att, tm=1024, tn=1024, out_dtype=MXU_DTYPE, col_scale=col_scale, name="proj_att")
    gn = _matmul(h2d, w_gn, tm=1024, tn=LANES, out_dtype=F32, name="proj_gates")
    proj3 = proj.reshape(b, t, proj.shape[1])
    patt3 = patt.reshape(b, t, o_gn)

    kv_c = patt3[:, :, c_kv[0]:c_kv[2]].reshape(b, rows16, CMP_STRIDE, 2, g, dk)
    a_cmp = kv_c.transpose(0, 3, 4, 1, 2, 5).reshape(b, 2, g, rows16, CMP_STRIDE * dk)
    pos8 = jnp.broadcast_to(cmp_pos.reshape(2, 1, CMP_BLOCK * dk), (2, 8, CMP_BLOCK * dk))
    cmp_out = _compress(a_cmp, cmp_w1.astype(MXU_DTYPE), pos8, cmp_b1.reshape(2, 1, CMP_HIDDEN),
                        cmp_w2.astype(MXU_DTYPE), cmp_b2.reshape(2, 1, dk), n_cmp)
    cmp_pad = jnp.pad(cmp_out, ((0, 0), (0, 0), (0, 0), (CMP_FRONT, ncp - CMP_FRONT - rows16), (0, 0)))

    gt = gn[:, :NSA_HEADS * 3].reshape(b, nq, Q_BLOCK, g, r, 3).transpose(0, 3, 1, 5, 4, 2).reshape(b, g, nq, 3, QL)
    gt = jnp.pad(gt, ((0, 0), (0, 0), (0, 0), (0, 5), (0, 0)))
    bwin_full, bwin_shifted, bc, c31 = _bias_tiles(rel_table)
    bwin = _bias_variants(bwin_full, WINDOW // Q_BLOCK + 1)
    bnear = _bias_variants(bwin_shifted, 2)
    selt = jnp.asarray(_selection_map_t(n_cmp, n_slc, ncp), MXU_DTYPE)
    y_a = _nsa(patt3, cmp_pad, selt, c_kv[2] // dk, gt, bnear, bwin, bc, c31)

    y_b = _sgu(proj3, sgu_ln_g.reshape(1, SGU_WIDTH), sgu_ln_b.reshape(1, SGU_WIDTH), sgu_w, sgu_b.T)
    return proj, y_a, y_b, c_ga, c_gb


def _token_mixer(h, w_in, cmp_pos, cmp_w1, cmp_b1, cmp_w2, cmp_b2, rel_table,
                 sgu_ln_g, sgu_ln_b, sgu_w, sgu_b, w_proj_nsa, w_proj_sgu):
    proj, y_a, y_b, c_ga, c_gb = _mixer_branches(h, w_in, cmp_pos, cmp_w1, cmp_b1, cmp_w2, cmp_b2, rel_table,
                                                 sgu_ln_g, sgu_ln_b, sgu_w, sgu_b)
    m = proj.shape[0]
    tn = 1024
    return _merge(y_a.reshape(m, NSA_WIDTH), y_b.reshape(m, SGU_WIDTH),
                  w_proj_nsa.astype(MXU_DTYPE), w_proj_sgu.astype(MXU_DTYPE),
                  proj, c_ga // tn, c_gb // tn, tm=1024, tn=tn)


def kernel(x, c, w_mod, b_mod, g_norms, w_in, cmp_pos, cmp_w1, cmp_b1, cmp_w2, cmp_b2, rel_table, sgu_ln_g, sgu_ln_b, sgu_w, sgu_b, w_proj_nsa, w_proj_sgu, w_out, w_ffn_up, ffn_conv_w, ffn_conv_b, w_ffn_down):
    b, t, d = x.shape
    m = b * t
    depth = w_mod.shape[0]
    c8 = jnp.pad(c, ((0, 8 - b), (0, 0)))
    for l in range(depth):
        mod = _mod(c8, w_mod[l], b_mod[l].reshape(1, -1))[:b]
        sh1, sc1, gt1, sh2, sc2, gt2 = [v.reshape(b, 1, d) for v in jnp.split(mod, 6, axis=-1)]
        gn = g_norms[l]

        h = _norm_mod(x, gn[0:1], sh1, sc1)
        merged = _token_mixer(h, w_in[l], cmp_pos[l], cmp_w1[l], cmp_b1[l], cmp_w2[l], cmp_b2[l], rel_table,
                              sgu_ln_g[l], sgu_ln_b[l], sgu_w[l], sgu_b[l], w_proj_nsa[l], w_proj_sgu[l])
        y = _matmul(merged, w_out[l].astype(MXU_DTYPE), tm=1024, tn=1024, out_dtype=MXU_DTYPE, name="out_proj")
        x1, h2 = _resid_norm_mod(x, y.reshape(b, t, d), gn[1:2], gt1, gn[2:3], sh2, sc2)

        cw8 = _pad_rows(ffn_conv_w[l], 8)
        cb8 = _pad_rows(ffn_conv_b[l].reshape(1, -1), 8)
        act = _ffn_up(h2.reshape(m, d), w_ffn_up[l].astype(MXU_DTYPE), cw8, cb8, seq=t)
        y2 = _matmul(act, w_ffn_down[l].astype(MXU_DTYPE), tm=512, tn=512, out_dtype=MXU_DTYPE,
                     name="ffn_down")
        x = _resid_norm(x1, y2.reshape(b, t, d), gn[3:4], gt2)
    return x
```

```python
import functools
import math

import jax
import jax.numpy as jnp
import numpy as np
from jax.experimental import pallas as pl
from jax.experimental.pallas import tpu as pltpu

NSA_HEADS = 16
NSA_KV_GROUPS = 4
HEADS_PER_GROUP = NSA_HEADS // NSA_KV_GROUPS
HEAD_DIM = 128
CMP_BLOCK = 32
CMP_STRIDE = 16
CMP_HIDDEN = 256
SLC_BLOCK = 64
SLC_TOPK = 16
WINDOW = 512
Q_BLOCK = 256
SGU_GROUPS = 16
SGU_GROUP_DIM = 128
SGU_CHUNK = 128
SGU_WIDTH = SGU_GROUPS * SGU_GROUP_DIM
REL_BUCKETS = 32
REL_MAX_DIST = 128
CONV_WIDTH = 3
EPS = 1e-6
NEG = -1e30
FORCE = 1e6
LOG2E = math.log2(math.e)

NSA_WIDTH = NSA_HEADS * HEAD_DIM
KV_WIDTH = NSA_KV_GROUPS * HEAD_DIM
LANES = 128
QL = HEADS_PER_GROUP * Q_BLOCK
GROUP_WIDTH = HEADS_PER_GROUP * HEAD_DIM
CMP_FRONT = 16
CMP_NEAR = 32
FFN_HALF_TILE = 256
FAR_TILES = 4
FAR_PARTS = 2

MXU_DTYPE = jnp.bfloat16
F32 = jnp.float32
VMEM_LIMIT = 56 * 1024 * 1024


def _cparams(sem):
    return pltpu.CompilerParams(dimension_semantics=sem, vmem_limit_bytes=VMEM_LIMIT)


def _mod_kernel(c_ref, w_ref, b_ref, o_ref):
    c = c_ref[...]
    ca = c * jax.nn.sigmoid(c)
    o_ref[...] = jnp.dot(ca.astype(MXU_DTYPE), w_ref[...].astype(MXU_DTYPE),
                         preferred_element_type=F32) + b_ref[...]


def _mod(c8, w_mod, b_mod, tn=512):
    rows, d = c8.shape
    n = w_mod.shape[1]
    return pl.pallas_call(
        _mod_kernel,
        grid=(n // tn,),
        in_specs=[pl.BlockSpec((rows, d), lambda j: (0, 0)),
                  pl.BlockSpec((d, tn), lambda j: (0, j)),
                  pl.BlockSpec((1, tn), lambda j: (0, j))],
        out_specs=pl.BlockSpec((rows, tn), lambda j: (0, j)),
        out_shape=jax.ShapeDtypeStruct((rows, n), F32),
        compiler_params=_cparams(("arbitrary",)),
        name="mod",
    )(c8, w_mod, b_mod)


def _rms(x, g):
    return x * jax.lax.rsqrt(jnp.mean(x * x, axis=-1, keepdims=True) + EPS) * g


def _norm_mod_kernel(x_ref, g_ref, sh_ref, sc_ref, o_ref):
    y = _rms(x_ref[0], g_ref[...])
    o_ref[0] = (y * (1.0 + sc_ref[0]) + sh_ref[0]).astype(o_ref.dtype)


def _norm_mod(x, g, sh, sc, tm=256):
    b, t, d = x.shape
    row = pl.BlockSpec((1, 1, d), lambda i, j: (i, 0, 0))
    return pl.pallas_call(
        _norm_mod_kernel,
        grid=(b, t // tm),
        in_specs=[pl.BlockSpec((1, tm, d), lambda i, j: (i, j, 0)),
                  pl.BlockSpec((1, d), lambda i, j: (0, 0)), row, row],
        out_specs=pl.BlockSpec((1, tm, d), lambda i, j: (i, j, 0)),
        out_shape=jax.ShapeDtypeStruct((b, t, d), MXU_DTYPE),
        compiler_params=_cparams(("arbitrary", "arbitrary")),
        name="norm_mod",
    )(x, g, sh, sc)


def _resid_norm_mod_kernel(x_ref, y_ref, g1_ref, gt_ref, g2_ref, sh_ref, sc_ref, x1_ref, h_ref):
    x1 = x_ref[0] + gt_ref[0] * _rms(y_ref[0].astype(F32), g1_ref[...])
    x1_ref[0] = x1
    h_ref[0] = (_rms(x1, g2_ref[...]) * (1.0 + sc_ref[0]) + sh_ref[0]).astype(h_ref.dtype)


def _resid_norm_mod(x, y, g1, gt, g2, sh, sc, tm=256):
    b, t, d = x.shape
    row = pl.BlockSpec((1, 1, d), lambda i, j: (i, 0, 0))
    gspec = pl.BlockSpec((1, d), lambda i, j: (0, 0))
    blk = pl.BlockSpec((1, tm, d), lambda i, j: (i, j, 0))
    return pl.pallas_call(
        _resid_norm_mod_kernel,
        grid=(b, t // tm),
        in_specs=[blk, blk, gspec, row, gspec, row, row],
        out_specs=[blk, blk],
        out_shape=[jax.ShapeDtypeStruct((b, t, d), F32), jax.ShapeDtypeStruct((b, t, d), MXU_DTYPE)],
        compiler_params=_cparams(("arbitrary", "arbitrary")),
        name="resid_norm_mod",
    )(x, y, g1, gt, g2, sh, sc)


def _resid_norm_kernel(x_ref, y_ref, g_ref, gt_ref, o_ref):
    o_ref[0] = x_ref[0] + gt_ref[0] * _rms(y_ref[0].astype(F32), g_ref[...])


def _resid_norm(x, y, g, gt, tm=256):
    b, t, d = x.shape
    blk = pl.BlockSpec((1, tm, d), lambda i, j: (i, j, 0))
    return pl.pallas_call(
        _resid_norm_kernel,
        grid=(b, t // tm),
        in_specs=[blk, blk, pl.BlockSpec((1, d), lambda i, j: (0, 0)),
                  pl.BlockSpec((1, 1, d), lambda i, j: (i, 0, 0))],
        out_specs=blk,
        out_shape=jax.ShapeDtypeStruct((b, t, d), F32),
        compiler_params=_cparams(("arbitrary", "arbitrary")),
        name="resid_norm",
    )(x, y, g, gt)


def _mm_kernel(x_ref, w_ref, *rest):
    acc = jnp.dot(x_ref[...], w_ref[...], preferred_element_type=F32)
    if len(rest) == 2:
        acc = acc * rest[0][...]
    o_ref = rest[-1]
    o_ref[...] = acc.astype(o_ref.dtype)


def _matmul(x, w, *, tm, tn, out_dtype, col_scale=None, name="matmul"):
    m, kdim = x.shape
    n = w.shape[1]
    in_specs = [pl.BlockSpec((tm, kdim), lambda i, j: (i, 0)),
                pl.BlockSpec((kdim, tn), lambda i, j: (0, j))]
    args = [x, w]
    if col_scale is not None:
        in_specs.append(pl.BlockSpec((1, tn), lambda i, j: (0, j)))
        args.append(col_scale)
    return pl.pallas_call(
        _mm_kernel,
        grid=(m // tm, n // tn),
        in_specs=in_specs,
        out_specs=pl.BlockSpec((tm, tn), lambda i, j: (i, j)),
        out_shape=jax.ShapeDtypeStruct((m, n), out_dtype),
        compiler_params=_cparams(("arbitrary", "arbitrary")),
        name=name,
    )(*args)


def _cast_kernel(a_ref, *rest, shift):
    o_ref = rest[-1]
    if shift:
        b_ref = rest[0]
        o_ref[...] = jnp.concatenate([a_ref[:, shift:], b_ref[:, :shift]], axis=1).astype(o_ref.dtype)
    else:
        o_ref[...] = a_ref[...].astype(o_ref.dtype)


def _cast_columns(w, start, width, *, tc, tr=512):
    rows = w.shape[0]
    base = start // LANES * LANES
    shift = start - base
    assert base % tc == 0 and width % tc == 0 and rows % tr == 0
    in_specs = [pl.BlockSpec((tr, tc), lambda i, j: (i, base // tc + j))]
    if shift:
        in_specs.append(pl.BlockSpec((tr, LANES), lambda i, j: (i, (base + (j + 1) * tc) // LANES)))
    return pl.pallas_call(
        functools.partial(_cast_kernel, shift=shift),
        grid=(rows // tr, width // tc),
        in_specs=in_specs,
        out_specs=pl.BlockSpec((tr, tc), lambda i, j: (i, j)),
        out_shape=jax.ShapeDtypeStruct((rows, width), MXU_DTYPE),
        compiler_params=_cparams(("arbitrary", "arbitrary")),
        name="cast_columns",
    )(*([w] * len(in_specs)))


def _compress_kernel(a_ref, w1_ref, pos_ref, b1_ref, w2_ref, b2_ref, o_ref, *, n_cmp):
    a = a_ref[0, 0, 0]
    w1 = w1_ref[0]
    half = a.shape[1]
    top = jnp.dot(a, w1[:half], preferred_element_type=F32)
    bot = jnp.dot(a, w1[half:], preferred_element_type=F32)
    rows = a.shape[0]
    pre = top + pltpu.roll(bot, rows - 1, 0)
    pos_term = jnp.dot(pos_ref[0].astype(MXU_DTYPE), w1, preferred_element_type=F32)[0:1]
    hid = jax.nn.gelu(pre + pos_term + b1_ref[0])
    out = jnp.dot(hid.astype(MXU_DTYPE), w2_ref[0], preferred_element_type=F32) + b2_ref[0]
    ridx = jax.lax.broadcasted_iota(jnp.int32, out.shape, 0)
    o_ref[0, 0, 0] = jnp.where(ridx < n_cmp, out, 0.0)


def _compress(a, w1, pos8, b1, w2, b2, n_cmp):
    b, two, g, rows, width = a.shape
    hid = w1.shape[2]
    dk = w2.shape[2]
    return pl.pallas_call(
        functools.partial(_compress_kernel, n_cmp=n_cmp),
        grid=(two, b, g),
        in_specs=[pl.BlockSpec((1, 1, 1, rows, width), lambda s, i, j: (i, s, j, 0, 0)),
                  pl.BlockSpec((1, 2 * width, hid), lambda s, i, j: (s, 0, 0)),
                  pl.BlockSpec((1, 8, 2 * width), lambda s, i, j: (s, 0, 0)),
                  pl.BlockSpec((1, 1, hid), lambda s, i, j: (s, 0, 0)),
                  pl.BlockSpec((1, hid, dk), lambda s, i, j: (s, 0, 0)),
                  pl.BlockSpec((1, 1, dk), lambda s, i, j: (s, 0, 0))],
        out_specs=pl.BlockSpec((1, 1, 1, rows, dk), lambda s, i, j: (i, s, j, 0, 0)),
        out_shape=jax.ShapeDtypeStruct((b, two, g, rows, dk), F32),
        compiler_params=_cparams(("arbitrary", "arbitrary", "arbitrary")),
        name="compress",
    )(a, w1, pos8, b1, w2, b2)


def _rel_bucket_np(dist):
    n = np.maximum(dist, 0)
    max_exact = REL_BUCKETS // 2
    nf = np.maximum(n, 1).astype(np.float32)
    large = max_exact + (np.log(nf / max_exact) / math.log(REL_MAX_DIST / max_exact)
                         * (REL_BUCKETS - max_exact)).astype(np.int32)
    large = np.minimum(large, REL_BUCKETS - 1)
    return np.where(n < max_exact, n, large).astype(np.int32)


def _bias_index_tiles():
    w = np.arange(WINDOW + Q_BLOCK)[:, None]
    i = np.arange(Q_BLOCK)[None, :]
    d_w = i - w + WINDOW
    win = np.where((d_w >= 0) & (d_w < WINDOW), _rel_bucket_np(d_w), -1)
    m = np.arange(CMP_NEAR)[:, None]
    d_c = i - CMP_STRIDE * m + (CMP_STRIDE * CMP_FRONT - CMP_BLOCK + 1)
    near = np.where(d_c >= 0, _rel_bucket_np(d_c), -1)
    return win.astype(np.int32), near.astype(np.int32)


def _bias_kernel(tab_ref, idw_ref, idc_ref, ow_ref, os_ref, oc_ref, o31_ref):
    h = pl.program_id(0)
    far = tab_ref[REL_BUCKETS - 1, h] * LOG2E

    def lut(idx, shift):
        out = jnp.full(idx.shape, NEG, F32)
        for b in range(REL_BUCKETS):
            out = jnp.where(idx == b, tab_ref[b, h] * LOG2E - shift, out)
        return out

    ow_ref[0] = lut(idw_ref[...], 0.0)
    os_ref[0] = lut(idw_ref[...], far)
    oc_ref[0] = lut(idc_ref[...], 0.0)
    o31_ref[0] = jnp.full(o31_ref.shape[1:], far, F32)


def _bias_tiles(rel_table):
    win, near = _bias_index_tiles()
    g, r = NSA_KV_GROUPS, HEADS_PER_GROUP
    full = lambda a: pl.BlockSpec(a.shape, lambda h: (0, 0))
    head = lambda rows: pl.BlockSpec((1, rows, Q_BLOCK), lambda h: (h // r, 0, h % r))
    return pl.pallas_call(
        _bias_kernel,
        grid=(NSA_HEADS,),
        in_specs=[pl.BlockSpec(memory_space=pltpu.SMEM), full(win), full(near)],
        out_specs=[head(win.shape[0]), head(win.shape[0]), head(CMP_NEAR), head(8)],
        out_shape=[jax.ShapeDtypeStruct((g, win.shape[0], QL), F32), jax.ShapeDtypeStruct((g, win.shape[0], QL), F32),
                   jax.ShapeDtypeStruct((g, CMP_NEAR, QL), F32), jax.ShapeDtypeStruct((g, 8, QL), F32)],
        compiler_params=_cparams(("arbitrary",)),
        name="bias_tiles",
    )(rel_table, jnp.asarray(win), jnp.asarray(near))


def _bias_variants(bwin_full, n_tiles):
    g = bwin_full.shape[0]
    n_win = WINDOW // Q_BLOCK
    tiles = bwin_full.reshape(g, n_win + 1, Q_BLOCK, QL)
    masked = jnp.full((g, Q_BLOCK, QL), NEG, F32)
    out = []
    for v in range(n_tiles):
        chunk = [tiles[:, n_win - v + u] if v - u >= 0 else masked for u in range(n_tiles)]
        out.append(jnp.concatenate(chunk, axis=1))
    return jnp.stack(out, axis=1)


def _selection_map_t(n_cmp, n_slc, ncp):
    ratio = SLC_BLOCK // CMP_STRIDE
    span = CMP_BLOCK // CMP_STRIDE
    d = np.arange(n_cmp)[:, None] - ratio * np.arange(n_slc)[None, :]
    a = d[..., None] + np.arange(span)
    sel = np.sum((a >= 0) & (a < ratio), axis=-1).astype(np.float32)
    out = np.zeros((n_slc, ncp), np.float32)
    out[:, CMP_FRONT:CMP_FRONT + n_cmp] = sel.T
    return out


def _nsa_kernel(q_ref, kc_ref, vc_ref, selt_ref, ks_ref, vs_ref, kw_ref, vw_ref, g_ref,
                bnear_ref, bwin_ref, bc_ref, c31_ref, onehot_ref, o_ref,
                p_scr, selb_scr, selbf_scr, m_scr, l_scr, acc_scr, *, top_k):
    qb = pl.program_id(2)
    qn = q_ref[0]
    q = jnp.concatenate([qn[:, r * HEAD_DIM:(r + 1) * HEAD_DIM].astype(F32).T for r in range(HEADS_PER_GROUP)],
                        axis=1).astype(MXU_DTYPE)
    ncp = kc_ref.shape[3]
    n_slc = selt_ref.shape[0]
    c31 = c31_ref[0][0:1, :]
    lane_i = jax.lax.broadcasted_iota(jnp.int32, (1, QL), 1) & (Q_BLOCK - 1)
    t0 = qb * Q_BLOCK

    def vt_dot(v_rows, p):
        return jax.lax.dot_general(v_rows, p.astype(MXU_DTYPE), (((0,), (0,)), ((), ())),
                                   preferred_element_type=F32)

    def scores_softmax(s, v_rows):
        m_c = jnp.max(s, axis=0, keepdims=True)
        pe = jnp.exp2(s - m_c)
        return m_c, jnp.sum(pe, axis=0, keepdims=True), vt_dot(v_rows, pe)

    def keys(ref, tile, n_tiles):
        return ref[0, pl.ds(pl.multiple_of(tile * Q_BLOCK, Q_BLOCK), n_tiles * Q_BLOCK), :]

    def block_mask(scr, tile, n_tiles):
        blocks_per_tile = Q_BLOCK // SLC_BLOCK
        rows = [jnp.broadcast_to(scr[pl.ds(blocks_per_tile * tile + u, 1), :], (SLC_BLOCK, QL))
                for u in range(blocks_per_tile * n_tiles)]
        return jnp.concatenate(rows, axis=0)

    n_win = WINDOW // Q_BLOCK
    win_tile = jnp.maximum(qb - n_win, 0)
    near_tile = jnp.maximum(qb - 1, 0)
    s_all = jnp.dot(kc_ref[0, 0, 0].astype(MXU_DTYPE), q, preferred_element_type=F32)
    s_win = jnp.dot(keys(kw_ref, win_tile, n_win + 1), q, preferred_element_type=F32) + bwin_ref[0, 0]
    s_pair = jnp.dot(keys(ks_ref, near_tile, 2), q, preferred_element_type=F32) + bnear_ref[0, 0]

    prow = jax.lax.broadcasted_iota(jnp.int32, (ncp, QL), 0)
    near0 = pl.multiple_of(qb * (Q_BLOCK // CMP_STRIDE), 8)
    s_far = jnp.where(prow < near0, jnp.where(prow >= CMP_FRONT, s_all + c31, NEG), NEG)
    kn = kc_ref[0, 0, 0, pl.ds(near0, CMP_NEAR), :]
    mrow = jax.lax.broadcasted_iota(jnp.int32, (CMP_NEAR, QL), 0)
    s_near = jnp.dot(kn.astype(MXU_DTYPE), q, preferred_element_type=F32) + bc_ref[0]
    s_near = jnp.where(mrow >= CMP_FRONT - near0, s_near, NEG)
    mx = jnp.maximum(jnp.max(s_far, axis=0, keepdims=True), jnp.max(s_near, axis=0, keepdims=True))
    e_far = jnp.exp2(s_far - mx)
    e_near = jnp.exp2(s_near - mx)
    den = jnp.sum(e_far, axis=0, keepdims=True) + jnp.sum(e_near, axis=0, keepdims=True)
    inv = jnp.where(t0 + lane_i >= CMP_BLOCK - 1, 1.0 / den, 0.0)
    p_scr[...] = e_far * inv
    p_scr[pl.ds(near0, CMP_NEAR), :] = e_near * inv
    p = p_scr[...]
    o_c = vt_dot(vc_ref[0, 0, 0].astype(MXU_DTYPE), p)

    psum = p[:, 0:Q_BLOCK]
    for r in range(1, HEADS_PER_GROUP):
        psum = psum + p[:, r * Q_BLOCK:(r + 1) * Q_BLOCK]
    p_hi = psum.astype(MXU_DTYPE)
    p_lo = (psum - p_hi.astype(F32)).astype(MXU_DTYPE)
    selt = selt_ref[...]
    imp = (jnp.dot(selt, p_hi, preferred_element_type=F32)
           + jnp.dot(selt, p_lo, preferred_element_type=F32))

    _, l_w, acc_w = scores_softmax(s_win, keys(vw_ref, win_tile, n_win + 1))
    o_w = acc_w * (1.0 / l_w)

    jidx = jax.lax.broadcasted_iota(jnp.int32, (n_slc, Q_BLOCK), 0).astype(F32)
    iidx = jax.lax.broadcasted_iota(jnp.int32, (n_slc, Q_BLOCK), 1)
    cur = (qb * (Q_BLOCK // SLC_BLOCK) + jax.lax.shift_right_logical(iidx, int(math.log2(SLC_BLOCK)))).astype(F32)
    vis = jidx <= cur
    forced = jnp.where(jidx == 0.0, 1.0, jnp.where(jidx == cur, 1.0, jnp.where(jidx == cur - 1.0, 1.0, 0.0)))
    score = jnp.where(vis, jnp.where(forced > 0.5, FORCE, imp), -1.0)
    rem = score
    for _ in range(top_k):
        best = jnp.max(rem, axis=0, keepdims=True)
        first = jnp.min(jnp.where(rem == best, jidx, float(n_slc)), axis=0, keepdims=True)
        rem = jnp.where(jidx == first, -2.0, rem)
    n_far = jnp.maximum(qb - 1, 0)
    far_block = jidx < (n_far * (Q_BLOCK // SLC_BLOCK)).astype(F32)
    selb = jnp.where(vis, jnp.where(rem < -1.5, 0.0, NEG), NEG)
    selbf = jnp.where(far_block, selb, NEG)
    selb_scr[...] = jnp.concatenate([selb] * HEADS_PER_GROUP, axis=1)
    selbf_scr[...] = jnp.concatenate([selbf] * HEADS_PER_GROUP, axis=1)

    m0, l0, acc0 = scores_softmax(s_pair + block_mask(selb_scr, near_tile, 2), keys(vs_ref, near_tile, 2))
    m_scr[...] = m0
    l_scr[...] = l0
    acc_scr[...] = acc0

    far_blocks = FAR_TILES * Q_BLOCK // SLC_BLOCK
    pad_rows = jnp.zeros((HEAD_DIM - far_blocks, QL), MXU_DTYPE)

    def far_chunk(c, carry):
        mask_rows = selbf_scr[pl.ds(pl.multiple_of(c * far_blocks, far_blocks), far_blocks), :]
        q_aug = jnp.concatenate([q, mask_rows.astype(MXU_DTYPE), pad_rows], axis=0)
        part = FAR_TILES // FAR_PARTS
        scores = []
        for u in range(FAR_PARTS):
            k_aug = jnp.concatenate([keys(ks_ref, c * FAR_TILES + u * part, part),
                                     onehot_ref[u * part * Q_BLOCK:(u + 1) * part * Q_BLOCK, :]], axis=1)
            scores.append(jnp.dot(k_aug, q_aug, preferred_element_type=F32))
        stats = [scores_softmax(scores[u], keys(vs_ref, c * FAR_TILES + u * part, part)) for u in range(FAR_PARTS)]
        m_old = m_scr[...]
        m_new = m_old
        for m_c, _, _ in stats:
            m_new = jnp.maximum(m_new, m_c)
        a_old = jnp.exp2(m_old - m_new)
        l_new = a_old * l_scr[...]
        acc_new = a_old * acc_scr[...]
        for m_c, l_c, acc_c in stats:
            a_c = jnp.exp2(m_c - m_new)
            l_new = l_new + a_c * l_c
            acc_new = acc_new + a_c * acc_c
        l_scr[...] = l_new
        acc_scr[...] = acc_new
        m_scr[...] = m_new
        return carry

    jax.lax.fori_loop(0, (n_far + FAR_TILES - 1) // FAR_TILES, far_chunk, 0)
    o_s = acc_scr[...] * (1.0 / l_scr[...])

    gates = jax.nn.sigmoid(g_ref[0, 0, 0])
    o = gates[0:1, :] * o_c + gates[1:2, :] * o_s + gates[2:3, :] * o_w
    for r in range(HEADS_PER_GROUP):
        o_ref[0, :, r * HEAD_DIM:(r + 1) * HEAD_DIM] = o[:, r * Q_BLOCK:(r + 1) * Q_BLOCK].T.astype(o_ref.dtype)


def _nsa(att, cmp_pad, selt, kv_col, gt, bnear, bwin, bc, c31):
    b, seq, _ = att.shape
    g = cmp_pad.shape[2]
    nq = seq // Q_BLOCK
    ncp = cmp_pad.shape[3]
    n_slc = selt.shape[0]
    kv_block = lambda which: pl.BlockSpec((1, seq, HEAD_DIM), lambda i, j, k: (i, 0, kv_col + which * g + j))
    cmp_block = lambda which: pl.BlockSpec((1, 1, 1, ncp, HEAD_DIM), lambda i, j, k: (i, which, j, 0, 0))
    top_k = min(SLC_TOPK, n_slc)
    assert nq % FAR_TILES == 0 and nq > WINDOW // Q_BLOCK
    per_g = lambda rows: pl.BlockSpec((1, rows, QL), lambda i, j, k: (j, 0, 0))
    block_of_row = np.arange(FAR_TILES * Q_BLOCK)[:, None] // SLC_BLOCK
    onehot = jnp.asarray(block_of_row == np.arange(HEAD_DIM)[None, :], MXU_DTYPE)
    variant = lambda a: pl.BlockSpec((1, 1) + a.shape[2:], lambda i, j, k: (j, jnp.minimum(k, a.shape[1] - 1), 0, 0))
    return pl.pallas_call(
        functools.partial(_nsa_kernel, top_k=top_k),
        grid=(b, g, nq),
        in_specs=[pl.BlockSpec((1, Q_BLOCK, GROUP_WIDTH), lambda i, j, k: (i, k, j)),
                  cmp_block(0), cmp_block(1),
                  pl.BlockSpec((n_slc, ncp), lambda i, j, k: (0, 0)),
                  kv_block(0), kv_block(1), kv_block(2), kv_block(3),
                  pl.BlockSpec((1, 1, 1, 8, QL), lambda i, j, k: (i, j, k, 0, 0)),
                  variant(bnear), variant(bwin), per_g(CMP_NEAR), per_g(8),
                  pl.BlockSpec(onehot.shape, lambda i, j, k: (0, 0))],
        out_specs=pl.BlockSpec((1, Q_BLOCK, GROUP_WIDTH), lambda i, j, k: (i, k, j)),
        out_shape=jax.ShapeDtypeStruct((b, seq, g * GROUP_WIDTH), MXU_DTYPE),
        scratch_shapes=[pltpu.VMEM((ncp, QL), F32), pltpu.VMEM((n_slc, QL), F32), pltpu.VMEM((n_slc, QL), F32),
                        pltpu.VMEM((1, QL), F32), pltpu.VMEM((1, QL), F32), pltpu.VMEM((HEAD_DIM, QL), F32)],
        compiler_params=_cparams(("arbitrary", "arbitrary", "arbitrary")),
        name="nsa",
    )(att, cmp_pad, cmp_pad, selt, att, att, att, att, gt, bnear, bwin, bc, c31, onehot)


def _sgu_kernel(zu_ref, zv_ref, lg_ref, lb_ref, w_ref, bs_ref, o_ref):
    u = jax.nn.gelu(zu_ref[0].astype(F32))
    v = jax.nn.gelu(zv_ref[0].astype(F32))
    mu = jnp.mean(v, axis=-1, keepdims=True)
    var = jnp.mean(jnp.square(v - mu), axis=-1, keepdims=True)
    vn = ((v - mu) * jax.lax.rsqrt(var + EPS) * lg_ref[...] + lb_ref[...]).astype(MXU_DTYPE)
    tc = w_ref.shape[1]
    causal = (jax.lax.broadcasted_iota(jnp.int32, (tc, tc), 0)
              >= jax.lax.broadcasted_iota(jnp.int32, (tc, tc), 1))
    bs = bs_ref[...]
    for h in range(w_ref.shape[0]):
        sl = slice(h * SGU_GROUP_DIM, (h + 1) * SGU_GROUP_DIM)
        w = jnp.where(causal, w_ref[h], 0.0).astype(MXU_DTYPE)
        mixed = jnp.dot(w, vn[:, sl], preferred_element_type=F32) + bs[:, h:h + 1]
        o_ref[0, :, sl] = (u[:, sl] * mixed).astype(o_ref.dtype)


def _sgu(proj3, ln_g, ln_b, w_s, b_st):
    b, t, _ = proj3.shape
    gs, tc, _ = w_s.shape
    vec = pl.BlockSpec((1, SGU_WIDTH), lambda i, j: (0, 0))
    return pl.pallas_call(
        _sgu_kernel,
        grid=(b, t // tc),
        in_specs=[pl.BlockSpec((1, tc, SGU_WIDTH), lambda i, j: (i, j, 0)),
                  pl.BlockSpec((1, tc, SGU_WIDTH), lambda i, j: (i, j, 1)),
                  vec, vec,
                  pl.BlockSpec((gs, tc, tc), lambda i, j: (0, 0, 0)),
                  pl.BlockSpec((tc, gs), lambda i, j: (0, 0))],
        out_specs=pl.BlockSpec((1, tc, SGU_WIDTH), lambda i, j: (i, j, 0)),
        out_shape=jax.ShapeDtypeStruct((b, t, SGU_WIDTH), MXU_DTYPE),
        compiler_params=_cparams(("arbitrary", "arbitrary")),
        name="sgu",
    )(proj3, proj3, ln_g, ln_b, w_s, b_st)


def _merge_kernel(ya_ref, yb_ref, wa_ref, wb_ref, ga_ref, gb_ref, o_ref):
    pa = jnp.dot(ya_ref[...], wa_ref[...], preferred_element_type=F32)
    pb = jnp.dot(yb_ref[...], wb_ref[...], preferred_element_type=F32)
    merged = (jax.nn.sigmoid(ga_ref[...].astype(F32)) * pa + jax.nn.sigmoid(gb_ref[...].astype(F32)) * pb)
    o_ref[...] = merged.astype(o_ref.dtype)


def _merge(ya, yb, wa, wb, proj, ga_col, gb_col, *, tm=1024, tn=512):
    m, ka = ya.shape
    kb = yb.shape[1]
    n = wa.shape[1]
    return pl.pallas_call(
        _merge_kernel,
        grid=(m // tm, n // tn),
        in_specs=[pl.BlockSpec((tm, ka), lambda i, j: (i, 0)),
                  pl.BlockSpec((tm, kb), lambda i, j: (i, 0)),
                  pl.BlockSpec((ka, tn), lambda i, j: (0, j)),
                  pl.BlockSpec((kb, tn), lambda i, j: (0, j)),
                  pl.BlockSpec((tm, tn), lambda i, j: (i, ga_col + j)),
                  pl.BlockSpec((tm, tn), lambda i, j: (i, gb_col + j))],
        out_specs=pl.BlockSpec((tm, tn), lambda i, j: (i, j)),
        out_shape=jax.ShapeDtypeStruct((m, n), MXU_DTYPE),
        compiler_params=_cparams(("arbitrary", "arbitrary")),
        name="merge",
    )(ya, yb, wa, wb, proj, proj)


def _ffn_up_kernel(h_ref, halo_ref, wg_ref, wu_ref, cwg_ref, cwu_ref, cbg_ref, cbu_ref, o_ref, *, seq, sub):
    i = pl.program_id(0)
    tm = h_ref.shape[0]
    wg = wg_ref[...]
    wu = wu_ref[...]
    row = jax.lax.broadcasted_iota(jnp.int32, (sub, wg.shape[1]), 0)

    def conv(a, before, cw_ref, cb_ref):
        a1 = jnp.where(row == 0, before[7:8, :], pltpu.roll(a, 1, 0))
        a2 = jnp.where(row == 0, before[6:7, :], jnp.where(row == 1, before[7:8, :], pltpu.roll(a, 2, 0)))
        return cb_ref[0:1, :] + a2 * cw_ref[0:1, :] + a1 * cw_ref[1:2, :] + a * cw_ref[2:3, :]

    keep = jnp.where((i * tm) % seq == 0, 0.0, 1.0)
    before_g = jnp.dot(halo_ref[...], wg, preferred_element_type=F32) * keep
    before_u = jnp.dot(halo_ref[...], wu, preferred_element_type=F32) * keep
    for s in range(tm // sub):
        hs = h_ref[s * sub:(s + 1) * sub, :]
        ag = jnp.dot(hs, wg, preferred_element_type=F32)
        au = jnp.dot(hs, wu, preferred_element_type=F32)
        gate = conv(ag, before_g, cwg_ref, cbg_ref)
        up = conv(au, before_u, cwu_ref, cbu_ref)
        o_ref[s * sub:(s + 1) * sub, :] = (gate * jax.nn.sigmoid(gate) * up).astype(o_ref.dtype)
        before_g = ag[sub - 8:, :]
        before_u = au[sub - 8:, :]


def _ffn_up(h, w, cw8, cb8, *, seq, tm=2048, sub=1024):
    m, d = h.shape
    f = w.shape[1] // 2
    tn = FFN_HALF_TILE
    nj = f // tn
    halo_blocks = tm // 8
    assert seq % tm == 0 and tm % sub == 0
    gate_cols = lambda rows: pl.BlockSpec((rows, tn), lambda i, j: (0, j))
    up_cols = lambda rows: pl.BlockSpec((rows, tn), lambda i, j: (0, nj + j))
    return pl.pallas_call(
        functools.partial(_ffn_up_kernel, seq=seq, sub=sub),
        grid=(m // tm, nj),
        in_specs=[pl.BlockSpec((tm, d), lambda i, j: (i, 0)),
                  pl.BlockSpec((8, d), lambda i, j: (jnp.maximum(i * halo_blocks - 1, 0), 0)),
                  gate_cols(d), up_cols(d), gate_cols(8), up_cols(8), gate_cols(8), up_cols(8)],
        out_specs=pl.BlockSpec((tm, tn), lambda i, j: (i, j)),
        out_shape=jax.ShapeDtypeStruct((m, f), MXU_DTYPE),
        compiler_params=_cparams(("arbitrary", "arbitrary")),
        name="ffn_up",
    )(h, h, w, w, cw8, cw8, cb8, cb8)


def _pad_rows(a, rows):
    return jnp.pad(a, ((0, rows - a.shape[0]), (0, 0)))


def _mixer_branches(h, w_in, cmp_pos, cmp_w1, cmp_b1, cmp_w2, cmp_b2, rel_table,
                    sgu_ln_g, sgu_ln_b, sgu_w, sgu_b):
    b, t, d = h.shape
    g, r, dk = NSA_KV_GROUPS, HEADS_PER_GROUP, HEAD_DIM
    m = b * t
    nq = t // Q_BLOCK
    n_cmp = (t - CMP_BLOCK) // CMP_STRIDE + 1
    n_slc = t // SLC_BLOCK
    rows16 = t // CMP_STRIDE
    ncp = rows16 + LANES

    o_gn = NSA_WIDTH + 6 * KV_WIDTH
    o_z = o_gn + NSA_HEADS * 3
    w_att = _cast_columns(w_in, 0, o_gn, tc=1024)
    w_gn = _cast_columns(w_in, o_gn, LANES, tc=LANES)
    w_rest = _cast_columns(w_in, o_z, w_in.shape[1] - o_z, tc=1024)
    c_ga, c_gb = 2 * SGU_WIDTH, 2 * SGU_WIDTH + d
    c_kv = [NSA_WIDTH + i * KV_WIDTH for i in range(6)]
    col_scale = jnp.ones((1, o_gn), F32).at[:, :NSA_WIDTH].set(HEAD_DIM ** -0.5 * LOG2E)

    h2d = h.reshape(m, d)
    proj = _matmul(h2d, w_rest, tm=1024, tn=1024, out_dtype=MXU_DTYPE, name="proj_rest")
    patt = _matmul(h2d, w_att, tm=1024, tn=1024, out_dtype=MXU_DTYPE, col_scale=col_scale, name="proj_att")
    gn = _matmul(h2d, w_gn, tm=1024, tn=LANES, out_dtype=F32, name="proj_gates")
    proj3 = proj.reshape(b, t, proj.shape[1])
    patt3 = patt.reshape(b, t, o_gn)

    kv_c = patt3[:, :, c_kv[0]:c_kv[2]].reshape(b, rows16, CMP_STRIDE, 2, g, dk)
    a_cmp = kv_c.transpose(0, 3, 4, 1, 2, 5).reshape(b, 2, g, rows16, CMP_STRIDE * dk)
    pos8 = jnp.broadcast_to(cmp_pos.reshape(2, 1, CMP_BLOCK * dk), (2, 8, CMP_BLOCK * dk))
    cmp_out = _compress(a_cmp, cmp_w1.astype(MXU_DTYPE), pos8, cmp_b1.reshape(2, 1, CMP_HIDDEN),
                        cmp_w2.astype(MXU_DTYPE), cmp_b2.reshape(2, 1, dk), n_cmp)
    cmp_pad = jnp.pad(cmp_out, ((0, 0), (0, 0), (0, 0), (CMP_FRONT, ncp - CMP_FRONT - rows16), (0, 0)))

    gt = gn[:, :NSA_HEADS * 3].reshape(b, nq, Q_BLOCK, g, r, 3).transpose(0, 3, 1, 5, 4, 2).reshape(b, g, nq, 3, QL)
    gt = jnp.pad(gt, ((0, 0), (0, 0), (0, 0), (0, 5), (0, 0)))
    bwin_full, bwin_shifted, bc, c31 = _bias_tiles(rel_table)
    bwin = _bias_variants(bwin_full, WINDOW // Q_BLOCK + 1)
    bnear = _bias_variants(bwin_shifted, 2)
    selt = jnp.asarray(_selection_map_t(n_cmp, n_slc, ncp), MXU_DTYPE)
    y_a = _nsa(patt3, cmp_pad, selt, c_kv[2] // dk, gt, bnear, bwin, bc, c31)

    y_b = _sgu(proj3, sgu_ln_g.reshape(1, SGU_WIDTH), sgu_ln_b.reshape(1, SGU_WIDTH), sgu_w, sgu_b.T)
    return proj, y_a, y_b, c_ga, c_gb


def _token_mixer(h, w_in, cmp_pos, cmp_w1, cmp_b1, cmp_w2, cmp_b2, rel_table,
                 sgu_ln_g, sgu_ln_b, sgu_w, sgu_b, w_proj_nsa, w_proj_sgu):
    proj, y_a, y_b, c_ga, c_gb = _mixer_branches(h, w_in, cmp_pos, cmp_w1, cmp_b1, cmp_w2, cmp_b2, rel_table,
                                                 sgu_ln_g, sgu_ln_b, sgu_w, sgu_b)
    m = proj.shape[0]
    tn = 1024
    return _merge(y_a.reshape(m, NSA_WIDTH), y_b.reshape(m, SGU_WIDTH),
                  w_proj_nsa.astype(MXU_DTYPE), w_proj_sgu.astype(MXU_DTYPE),
                  proj, c_ga // tn, c_gb // tn, tm=1024, tn=tn)


def kernel(x, c, w_mod, b_mod, g_norms, w_in, cmp_pos, cmp_w1, cmp_b1, cmp_w2, cmp_b2, rel_table, sgu_ln_g, sgu_ln_b, sgu_w, sgu_b, w_proj_nsa, w_proj_sgu, w_out, w_ffn_up, ffn_conv_w, ffn_conv_b, w_ffn_down):
    b, t, d = x.shape
    m = b * t
    depth = w_mod.shape[0]
    c8 = jnp.pad(c, ((0, 8 - b), (0, 0)))
    for l in range(depth):
        mod = _mod(c8, w_mod[l], b_mod[l].reshape(1, -1))[:b]
        sh1, sc1, gt1, sh2, sc2, gt2 = [v.reshape(b, 1, d) for v in jnp.split(mod, 6, axis=-1)]
        gn = g_norms[l]

        h = _norm_mod(x, gn[0:1], sh1, sc1)
        merged = _token_mixer(h, w_in[l], cmp_pos[l], cmp_w1[l], cmp_b1[l], cmp_w2[l], cmp_b2[l], rel_table,
                              sgu_ln_g[l], sgu_ln_b[l], sgu_w[l], sgu_b[l], w_proj_nsa[l], w_proj_sgu[l])
        y = _matmul(merged, w_out[l].astype(MXU_DTYPE), tm=1024, tn=1024, out_dtype=MXU_DTYPE, name="out_proj")
        x1, h2 = _resid_norm_mod(x, y.reshape(b, t, d), gn[1:2], gt1, gn[2:3], sh2, sc2)

        cw8 = _pad_rows(ffn_conv_w[l], 8)
        cb8 = _pad_rows(ffn_conv_b[l].reshape(1, -1), 8)
        act = _ffn_up(h2.reshape(m, d), w_ffn_up[l].astype(MXU_DTYPE), cw8, cb8, seq=t)
        y2 = _matmul(act, w_ffn_down[l].astype(MXU_DTYPE), tm=512, tn=512, out_dtype=MXU_DTYPE,
                     name="ffn_down")
        x = _resid_norm(x1, y2.reshape(b, t, d), gn[3:4], gt2)
    return x
```

```python
import functools
import math

import jax
import jax.numpy as jnp
import numpy as np
from jax.experimental import pallas as pl
from jax.experimental.pallas import tpu as pltpu

NSA_HEADS = 16
NSA_KV_GROUPS = 4
HEADS_PER_GROUP = NSA_HEADS // NSA_KV_GROUPS
HEAD_DIM = 128
CMP_BLOCK = 32
CMP_STRIDE = 16
CMP_HIDDEN = 256
SLC_BLOCK = 64
SLC_TOPK = 16
WINDOW = 512
Q_BLOCK = 256
SGU_GROUPS = 16
SGU_GROUP_DIM = 128
SGU_CHUNK = 128
SGU_WIDTH = SGU_GROUPS * SGU_GROUP_DIM
REL_BUCKETS = 32
REL_MAX_DIST = 128
CONV_WIDTH = 3
EPS = 1e-6
NEG = -1e30
FORCE = 1e6
LOG2E = math.log2(math.e)

NSA_WIDTH = NSA_HEADS * HEAD_DIM
KV_WIDTH = NSA_KV_GROUPS * HEAD_DIM
LANES = 128
QL = HEADS_PER_GROUP * Q_BLOCK
GROUP_WIDTH = HEADS_PER_GROUP * HEAD_DIM
CMP_FRONT = 16
CMP_NEAR = 32
FFN_HALF_TILE = 256
FAR_TILES = 4
FAR_PARTS = 2

MXU_DTYPE = jnp.bfloat16
F32 = jnp.float32
VMEM_LIMIT = 56 * 1024 * 1024


def _cparams(sem):
    return pltpu.CompilerParams(dimension_semantics=sem, vmem_limit_bytes=VMEM_LIMIT)


def _mod_kernel(c_ref, w_ref, b_ref, o_ref):
    c = c_ref[...]
    ca = c * jax.nn.sigmoid(c)
    o_ref[...] = jnp.dot(ca.astype(MXU_DTYPE), w_ref[...].astype(MXU_DTYPE),
                         preferred_element_type=F32) + b_ref[...]


def _mod(c8, w_mod, b_mod, tn=512):
    rows, d = c8.shape
    n = w_mod.shape[1]
    return pl.pallas_call(
        _mod_kernel,
        grid=(n // tn,),
        in_specs=[pl.BlockSpec((rows, d), lambda j: (0, 0)),
                  pl.BlockSpec((d, tn), lambda j: (0, j)),
                  pl.BlockSpec((1, tn), lambda j: (0, j))],
        out_specs=pl.BlockSpec((rows, tn), lambda j: (0, j)),
        out_shape=jax.ShapeDtypeStruct((rows, n), F32),
        compiler_params=_cparams(("arbitrary",)),
        name="mod",
    )(c8, w_mod, b_mod)


def _rms(x, g):
    return x * jax.lax.rsqrt(jnp.mean(x * x, axis=-1, keepdims=True) + EPS) * g


def _norm_mod_kernel(x_ref, g_ref, sh_ref, sc_ref, o_ref):
    y = _rms(x_ref[0], g_ref[...])
    o_ref[0] = (y * (1.0 + sc_ref[0]) + sh_ref[0]).astype(o_ref.dtype)


def _norm_mod(x, g, sh, sc, tm=256):
    b, t, d = x.shape
    row = pl.BlockSpec((1, 1, d), lambda i, j: (i, 0, 0))
    return pl.pallas_call(
        _norm_mod_kernel,
        grid=(b, t // tm),
        in_specs=[pl.BlockSpec((1, tm, d), lambda i, j: (i, j, 0)),
                  pl.BlockSpec((1, d), lambda i, j: (0, 0)), row, row],
        out_specs=pl.BlockSpec((1, tm, d), lambda i, j: (i, j, 0)),
        out_shape=jax.ShapeDtypeStruct((b, t, d), MXU_DTYPE),
        compiler_params=_cparams(("arbitrary", "arbitrary")),
        name="norm_mod",
    )(x, g, sh, sc)


def _resid_norm_mod_kernel(x_ref, y_ref, g1_ref, gt_ref, g2_ref, sh_ref, sc_ref, x1_ref, h_ref):
    x1 = x_ref[0] + gt_ref[0] * _rms(y_ref[0].astype(F32), g1_ref[...])
    x1_ref[0] = x1
    h_ref[0] = (_rms(x1, g2_ref[...]) * (1.0 + sc_ref[0]) + sh_ref[0]).astype(h_ref.dtype)


def _resid_norm_mod(x, y, g1, gt, g2, sh, sc, tm=256):
    b, t, d = x.shape
    row = pl.BlockSpec((1, 1, d), lambda i, j: (i, 0, 0))
    gspec = pl.BlockSpec((1, d), lambda i, j: (0, 0))
    blk = pl.BlockSpec((1, tm, d), lambda i, j: (i, j, 0))
    return pl.pallas_call(
        _resid_norm_mod_kernel,
        grid=(b, t // tm),
        in_specs=[blk, blk, gspec, row, gspec, row, row],
        out_specs=[blk, blk],
        out_shape=[jax.ShapeDtypeStruct((b, t, d), F32), jax.ShapeDtypeStruct((b, t, d), MXU_DTYPE)],
        compiler_params=_cparams(("arbitrary", "arbitrary")),
        name="resid_norm_mod",
    )(x, y, g1, gt, g2, sh, sc)


def _resid_norm_kernel(x_ref, y_ref, g_ref, gt_ref, o_ref):
    o_ref[0] = x_ref[0] + gt_ref[0] * _rms(y_ref[0].astype(F32), g_ref[...])


def _resid_norm(x, y, g, gt, tm=256):
    b, t, d = x.shape
    blk = pl.BlockSpec((1, tm, d), lambda i, j: (i, j, 0))
    return pl.pallas_call(
        _resid_norm_kernel,
        grid=(b, t // tm),
        in_specs=[blk, blk, pl.BlockSpec((1, d), lambda i, j: (0, 0)),
                  pl.BlockSpec((1, 1, d), lambda i, j: (i, 0, 0))],
        out_specs=blk,
        out_shape=jax.ShapeDtypeStruct((b, t, d), F32),
        compiler_params=_cparams(("arbitrary", "arbitrary")),
        name="resid_norm",
    )(x, y, g, gt)


def _mm_kernel(x_ref, w_ref, *rest, w_is_nk):
    contract_w = 1 if w_is_nk else 0
    acc = jax.lax.dot_general(x_ref[...], w_ref[...], (((1,), (contract_w,)), ((), ())),
                              preferred_element_type=F32)
    if len(rest) == 2:
        acc = acc * rest[0][...]
    o_ref = rest[-1]
    o_ref[...] = acc.astype(o_ref.dtype)


def _matmul(x, w, *, tm, tn, out_dtype, col_scale=None, w_is_nk=False, name="matmul"):
    m, kdim = x.shape
    n = w.shape[0] if w_is_nk else w.shape[1]
    w_spec = (pl.BlockSpec((tn, kdim), lambda i, j: (j, 0)) if w_is_nk
              else pl.BlockSpec((kdim, tn), lambda i, j: (0, j)))
    in_specs = [pl.BlockSpec((tm, kdim), lambda i, j: (i, 0)), w_spec]
    args = [x, w]
    if col_scale is not None:
        in_specs.append(pl.BlockSpec((1, tn), lambda i, j: (0, j)))
        args.append(col_scale)
    return pl.pallas_call(
        functools.partial(_mm_kernel, w_is_nk=w_is_nk),
        grid=(m // tm, n // tn),
        in_specs=in_specs,
        out_specs=pl.BlockSpec((tm, tn), lambda i, j: (i, j)),
        out_shape=jax.ShapeDtypeStruct((m, n), out_dtype),
        compiler_params=_cparams(("arbitrary", "arbitrary")),
        name=name,
    )(*args)


def _compress_kernel(a_ref, w1_ref, pos_ref, b1_ref, w2_ref, b2_ref, o_ref, *, n_cmp):
    a = a_ref[0, 0, 0]
    w1 = w1_ref[0]
    half = a.shape[1]
    top = jnp.dot(a, w1[:half], preferred_element_type=F32)
    bot = jnp.dot(a, w1[half:], preferred_element_type=F32)
    rows = a.shape[0]
    pre = top + pltpu.roll(bot, rows - 1, 0)
    pos_term = jnp.dot(pos_ref[0].astype(MXU_DTYPE), w1, preferred_element_type=F32)[0:1]
    hid = jax.nn.gelu(pre + pos_term + b1_ref[0])
    out = jnp.dot(hid.astype(MXU_DTYPE), w2_ref[0], preferred_element_type=F32) + b2_ref[0]
    ridx = jax.lax.broadcasted_iota(jnp.int32, out.shape, 0)
    o_ref[0, 0, 0] = jnp.where(ridx < n_cmp, out, 0.0)


def _compress(a, w1, pos8, b1, w2, b2, n_cmp):
    b, two, g, rows, width = a.shape
    hid = w1.shape[2]
    dk = w2.shape[2]
    return pl.pallas_call(
        functools.partial(_compress_kernel, n_cmp=n_cmp),
        grid=(two, b, g),
        in_specs=[pl.BlockSpec((1, 1, 1, rows, width), lambda s, i, j: (i, s, j, 0, 0)),
                  pl.BlockSpec((1, 2 * width, hid), lambda s, i, j: (s, 0, 0)),
                  pl.BlockSpec((1, 8, 2 * width), lambda s, i, j: (s, 0, 0)),
                  pl.BlockSpec((1, 1, hid), lambda s, i, j: (s, 0, 0)),
                  pl.BlockSpec((1, hid, dk), lambda s, i, j: (s, 0, 0)),
                  pl.BlockSpec((1, 1, dk), lambda s, i, j: (s, 0, 0))],
        out_specs=pl.BlockSpec((1, 1, 1, rows, dk), lambda s, i, j: (i, s, j, 0, 0)),
        out_shape=jax.ShapeDtypeStruct((b, two, g, rows, dk), F32),
        compiler_params=_cparams(("arbitrary", "arbitrary", "arbitrary")),
        name="compress",
    )(a, w1, pos8, b1, w2, b2)


def _rel_bucket_np(dist):
    n = np.maximum(dist, 0)
    max_exact = REL_BUCKETS // 2
    nf = np.maximum(n, 1).astype(np.float32)
    large = max_exact + (np.log(nf / max_exact) / math.log(REL_MAX_DIST / max_exact)
                         * (REL_BUCKETS - max_exact)).astype(np.int32)
    large = np.minimum(large, REL_BUCKETS - 1)
    return np.where(n < max_exact, n, large).astype(np.int32)


def _bias_index_tiles():
    w = np.arange(WINDOW + Q_BLOCK)[:, None]
    i = np.arange(Q_BLOCK)[None, :]
    d_w = i - w + WINDOW
    win = np.where((d_w >= 0) & (d_w < WINDOW), _rel_bucket_np(d_w), -1)
    m = np.arange(CMP_NEAR)[:, None]
    d_c = i - CMP_STRIDE * m + (CMP_STRIDE * CMP_FRONT - CMP_BLOCK + 1)
    near = np.where(d_c >= 0, _rel_bucket_np(d_c), -1)
    return win.astype(np.int32), near.astype(np.int32)


def _bias_kernel(tab_ref, idw_ref, idc_ref, ow_ref, os_ref, oc_ref, o31_ref):
    h = pl.program_id(0)
    far = tab_ref[REL_BUCKETS - 1, h] * LOG2E

    def lut(idx, shift):
        out = jnp.full(idx.shape, NEG, F32)
        for b in range(REL_BUCKETS):
            out = jnp.where(idx == b, tab_ref[b, h] * LOG2E - shift, out)
        return out

    ow_ref[0] = lut(idw_ref[...], 0.0)
    os_ref[0] = lut(idw_ref[...], far)
    oc_ref[0] = lut(idc_ref[...], 0.0)
    o31_ref[0] = jnp.full(o31_ref.shape[1:], far, F32)


def _bias_tiles(rel_table):
    win, near = _bias_index_tiles()
    g, r = NSA_KV_GROUPS, HEADS_PER_GROUP
    full = lambda a: pl.BlockSpec(a.shape, lambda h: (0, 0))
    head = lambda rows: pl.BlockSpec((1, rows, Q_BLOCK), lambda h: (h // r, 0, h % r))
    return pl.pallas_call(
        _bias_kernel,
        grid=(NSA_HEADS,),
        in_specs=[pl.BlockSpec(memory_space=pltpu.SMEM), full(win), full(near)],
        out_specs=[head(win.shape[0]), head(win.shape[0]), head(CMP_NEAR), head(8)],
        out_shape=[jax.ShapeDtypeStruct((g, win.shape[0], QL), F32), jax.ShapeDtypeStruct((g, win.shape[0], QL), F32),
                   jax.ShapeDtypeStruct((g, CMP_NEAR, QL), F32), jax.ShapeDtypeStruct((g, 8, QL), F32)],
        compiler_params=_cparams(("arbitrary",)),
        name="bias_tiles",
    )(rel_table, jnp.asarray(win), jnp.asarray(near))


def _bias_variants(bwin_full, n_tiles):
    g = bwin_full.shape[0]
    n_win = WINDOW // Q_BLOCK
    tiles = bwin_full.reshape(g, n_win + 1, Q_BLOCK, QL)
    masked = jnp.full((g, Q_BLOCK, QL), NEG, F32)
    out = []
    for v in range(n_tiles):
        chunk = [tiles[:, n_win - v + u] if v - u >= 0 else masked for u in range(n_tiles)]
        out.append(jnp.concatenate(chunk, axis=1))
    return jnp.stack(out, axis=1)


def _selection_map_t(n_cmp, n_slc, ncp):
    ratio = SLC_BLOCK // CMP_STRIDE
    span = CMP_BLOCK // CMP_STRIDE
    d = np.arange(n_cmp)[:, None] - ratio * np.arange(n_slc)[None, :]
    a = d[..., None] + np.arange(span)
    sel = np.sum((a >= 0) & (a < ratio), axis=-1).astype(np.float32)
    out = np.zeros((n_slc, ncp), np.float32)
    out[:, CMP_FRONT:CMP_FRONT + n_cmp] = sel.T
    return out


def _nsa_kernel(q_ref, kc_ref, vc_ref, selt_ref, ks_ref, vs_ref, kw_ref, vw_ref, g_ref,
                bnear_ref, bwin_ref, bc_ref, c31_ref, onehot_ref, o_ref,
                p_scr, selb_scr, selbf_scr, m_scr, l_scr, acc_scr, *, top_k):
    qb = pl.program_id(2)
    qn = q_ref[0]
    q = jnp.concatenate([qn[:, r * HEAD_DIM:(r + 1) * HEAD_DIM].astype(F32).T for r in range(HEADS_PER_GROUP)],
                        axis=1).astype(MXU_DTYPE)
    ncp = kc_ref.shape[3]
    n_slc = selt_ref.shape[0]
    c31 = c31_ref[0][0:1, :]
    lane_i = jax.lax.broadcasted_iota(jnp.int32, (1, QL), 1) & (Q_BLOCK - 1)
    t0 = qb * Q_BLOCK

    def vt_dot(v_rows, p):
        return jax.lax.dot_general(v_rows, p.astype(MXU_DTYPE), (((0,), (0,)), ((), ())),
                                   preferred_element_type=F32)

    def scores_softmax(s, v_rows):
        m_c = jnp.max(s, axis=0, keepdims=True)
        pe = jnp.exp2(s - m_c)
        return m_c, jnp.sum(pe, axis=0, keepdims=True), vt_dot(v_rows, pe)

    def keys(ref, tile, n_tiles):
        return ref[0, pl.ds(pl.multiple_of(tile * Q_BLOCK, Q_BLOCK), n_tiles * Q_BLOCK), :]

    def block_mask(scr, tile, n_tiles):
        blocks_per_tile = Q_BLOCK // SLC_BLOCK
        rows = [jnp.broadcast_to(scr[pl.ds(blocks_per_tile * tile + u, 1), :], (SLC_BLOCK, QL))
                for u in range(blocks_per_tile * n_tiles)]
        return jnp.concatenate(rows, axis=0)

    n_win = WINDOW // Q_BLOCK
    win_tile = jnp.maximum(qb - n_win, 0)
    near_tile = jnp.maximum(qb - 1, 0)
    s_all = jnp.dot(kc_ref[0, 0, 0].astype(MXU_DTYPE), q, preferred_element_type=F32)
    s_win = jnp.dot(keys(kw_ref, win_tile, n_win + 1), q, preferred_element_type=F32) + bwin_ref[0, 0]
    s_pair = jnp.dot(keys(ks_ref, near_tile, 2), q, preferred_element_type=F32) + bnear_ref[0, 0]

    prow = jax.lax.broadcasted_iota(jnp.int32, (ncp, QL), 0)
    near0 = pl.multiple_of(qb * (Q_BLOCK // CMP_STRIDE), 8)
    s_far = jnp.where(prow < near0, jnp.where(prow >= CMP_FRONT, s_all + c31, NEG), NEG)
    kn = kc_ref[0, 0, 0, pl.ds(near0, CMP_NEAR), :]
    mrow = jax.lax.broadcasted_iota(jnp.int32, (CMP_NEAR, QL), 0)
    s_near = jnp.dot(kn.astype(MXU_DTYPE), q, preferred_element_type=F32) + bc_ref[0]
    s_near = jnp.where(mrow >= CMP_FRONT - near0, s_near, NEG)
    mx = jnp.maximum(jnp.max(s_far, axis=0, keepdims=True), jnp.max(s_near, axis=0, keepdims=True))
    e_far = jnp.exp2(s_far - mx)
    e_near = jnp.exp2(s_near - mx)
    den = jnp.sum(e_far, axis=0, keepdims=True) + jnp.sum(e_near, axis=0, keepdims=True)
    inv = jnp.where(t0 + lane_i >= CMP_BLOCK - 1, 1.0 / den, 0.0)
    p_scr[...] = e_far * inv
    p_scr[pl.ds(near0, CMP_NEAR), :] = e_near * inv
    p = p_scr[...]
    o_c = vt_dot(vc_ref[0, 0, 0].astype(MXU_DTYPE), p)

    psum = p[:, 0:Q_BLOCK]
    for r in range(1, HEADS_PER_GROUP):
        psum = psum + p[:, r * Q_BLOCK:(r + 1) * Q_BLOCK]
    p_hi = psum.astype(MXU_DTYPE)
    p_lo = (psum - p_hi.astype(F32)).astype(MXU_DTYPE)
    selt = selt_ref[...]
    imp = (jnp.dot(selt, p_hi, preferred_element_type=F32)
           + jnp.dot(selt, p_lo, preferred_element_type=F32))

    _, l_w, acc_w = scores_softmax(s_win, keys(vw_ref, win_tile, n_win + 1))
    o_w = acc_w * (1.0 / l_w)

    jidx = jax.lax.broadcasted_iota(jnp.int32, (n_slc, Q_BLOCK), 0).astype(F32)
    iidx = jax.lax.broadcasted_iota(jnp.int32, (n_slc, Q_BLOCK), 1)
    cur = (qb * (Q_BLOCK // SLC_BLOCK) + jax.lax.shift_right_logical(iidx, int(math.log2(SLC_BLOCK)))).astype(F32)
    vis = jidx <= cur
    forced = jnp.where(jidx == 0.0, 1.0, jnp.where(jidx == cur, 1.0, jnp.where(jidx == cur - 1.0, 1.0, 0.0)))
    score = jnp.where(vis, jnp.where(forced > 0.5, FORCE, imp), -1.0)
    rem = score
    for _ in range(top_k):
        best = jnp.max(rem, axis=0, keepdims=True)
        first = jnp.min(jnp.where(rem == best, jidx, float(n_slc)), axis=0, keepdims=True)
        rem = jnp.where(jidx == first, -2.0, rem)
    n_far = jnp.maximum(qb - 1, 0)
    far_block = jidx < (n_far * (Q_BLOCK // SLC_BLOCK)).astype(F32)
    selb = jnp.where(vis, jnp.where(rem < -1.5, 0.0, NEG), NEG)
    selbf = jnp.where(far_block, selb, NEG)
    selb_scr[...] = jnp.concatenate([selb] * HEADS_PER_GROUP, axis=1)
    selbf_scr[...] = jnp.concatenate([selbf] * HEADS_PER_GROUP, axis=1)

    m0, l0, acc0 = scores_softmax(s_pair + block_mask(selb_scr, near_tile, 2), keys(vs_ref, near_tile, 2))
    m_scr[...] = m0
    l_scr[...] = l0
    acc_scr[...] = acc0

    far_blocks = FAR_TILES * Q_BLOCK // SLC_BLOCK
    pad_rows = jnp.zeros((HEAD_DIM - far_blocks, QL), MXU_DTYPE)

    def far_chunk(c, carry):
        mask_rows = selbf_scr[pl.ds(pl.multiple_of(c * far_blocks, far_blocks), far_blocks), :]
        q_aug = jnp.concatenate([q, mask_rows.astype(MXU_DTYPE), pad_rows], axis=0)
        part = FAR_TILES // FAR_PARTS
        scores = []
        for u in range(FAR_PARTS):
            k_aug = jnp.concatenate([keys(ks_ref, c * FAR_TILES + u * part, part),
                                     onehot_ref[u * part * Q_BLOCK:(u + 1) * part * Q_BLOCK, :]], axis=1)
            scores.append(jnp.dot(k_aug, q_aug, preferred_element_type=F32))
        stats = [scores_softmax(scores[u], keys(vs_ref, c * FAR_TILES + u * part, part)) for u in range(FAR_PARTS)]
        m_old = m_scr[...]
        m_new = m_old
        for m_c, _, _ in stats:
            m_new = jnp.maximum(m_new, m_c)
        a_old = jnp.exp2(m_old - m_new)
        l_new = a_old * l_scr[...]
        acc_new = a_old * acc_scr[...]
        for m_c, l_c, acc_c in stats:
            a_c = jnp.exp2(m_c - m_new)
            l_new = l_new + a_c * l_c
            acc_new = acc_new + a_c * acc_c
        l_scr[...] = l_new
        acc_scr[...] = acc_new
        m_scr[...] = m_new
        return carry

    jax.lax.fori_loop(0, (n_far + FAR_TILES - 1) // FAR_TILES, far_chunk, 0)
    o_s = acc_scr[...] * (1.0 / l_scr[...])

    gates = jax.nn.sigmoid(g_ref[0, 0, 0])
    o = gates[0:1, :] * o_c + gates[1:2, :] * o_s + gates[2:3, :] * o_w
    for r in range(HEADS_PER_GROUP):
        o_ref[0, :, r * HEAD_DIM:(r + 1) * HEAD_DIM] = o[:, r * Q_BLOCK:(r + 1) * Q_BLOCK].T.astype(o_ref.dtype)


def _nsa(att, cmp_pad, selt, kv_col, gt, bnear, bwin, bc, c31):
    b, seq, _ = att.shape
    g = cmp_pad.shape[2]
    nq = seq // Q_BLOCK
    ncp = cmp_pad.shape[3]
    n_slc = selt.shape[0]
    kv_block = lambda which: pl.BlockSpec((1, seq, HEAD_DIM), lambda i, j, k: (i, 0, kv_col + which * g + j))
    cmp_block = lambda which: pl.BlockSpec((1, 1, 1, ncp, HEAD_DIM), lambda i, j, k: (i, which, j, 0, 0))
    top_k = min(SLC_TOPK, n_slc)
    assert nq % FAR_TILES == 0 and nq > WINDOW // Q_BLOCK
    per_g = lambda rows: pl.BlockSpec((1, rows, QL), lambda i, j, k: (j, 0, 0))
    block_of_row = np.arange(FAR_TILES * Q_BLOCK)[:, None] // SLC_BLOCK
    onehot = jnp.asarray(block_of_row == np.arange(HEAD_DIM)[None, :], MXU_DTYPE)
    variant = lambda a: pl.BlockSpec((1, 1) + a.shape[2:], lambda i, j, k: (j, jnp.minimum(k, a.shape[1] - 1), 0, 0))
    return pl.pallas_call(
        functools.partial(_nsa_kernel, top_k=top_k),
        grid=(b, g, nq),
        in_specs=[pl.BlockSpec((1, Q_BLOCK, GROUP_WIDTH), lambda i, j, k: (i, k, j)),
                  cmp_block(0), cmp_block(1),
                  pl.BlockSpec((n_slc, ncp), lambda i, j, k: (0, 0)),
                  kv_block(0), kv_block(1), kv_block(2), kv_block(3),
                  pl.BlockSpec((1, 1, 1, 8, QL), lambda i, j, k: (i, j, k, 0, 0)),
                  variant(bnear), variant(bwin), per_g(CMP_NEAR), per_g(8),
                  pl.BlockSpec(onehot.shape, lambda i, j, k: (0, 0))],
        out_specs=pl.BlockSpec((1, Q_BLOCK, GROUP_WIDTH), lambda i, j, k: (i, k, j)),
        out_shape=jax.ShapeDtypeStruct((b, seq, g * GROUP_WIDTH), MXU_DTYPE),
        scratch_shapes=[pltpu.VMEM((ncp, QL), F32), pltpu.VMEM((n_slc, QL), F32), pltpu.VMEM((n_slc, QL), F32),
                        pltpu.VMEM((1, QL), F32), pltpu.VMEM((1, QL), F32), pltpu.VMEM((HEAD_DIM, QL), F32)],
        compiler_params=_cparams(("arbitrary", "arbitrary", "arbitrary")),
        name="nsa",
    )(att, cmp_pad, cmp_pad, selt, att, att, att, att, gt, bnear, bwin, bc, c31, onehot)


def _sgu_kernel(zu_ref, zv_ref, lg_ref, lb_ref, w_ref, bs_ref, o_ref):
    u = jax.nn.gelu(zu_ref[0].astype(F32))
    v = jax.nn.gelu(zv_ref[0].astype(F32))
    mu = jnp.mean(v, axis=-1, keepdims=True)
    var = jnp.mean(jnp.square(v - mu), axis=-1, keepdims=True)
    vn = ((v - mu) * jax.lax.rsqrt(var + EPS) * lg_ref[...] + lb_ref[...]).astype(MXU_DTYPE)
    tc = w_ref.shape[1]
    causal = (jax.lax.broadcasted_iota(jnp.int32, (tc, tc), 0)
              >= jax.lax.broadcasted_iota(jnp.int32, (tc, tc), 1))
    bs = bs_ref[...]
    for h in range(w_ref.shape[0]):
        sl = slice(h * SGU_GROUP_DIM, (h + 1) * SGU_GROUP_DIM)
        w = jnp.where(causal, w_ref[h], 0.0).astype(MXU_DTYPE)
        mixed = jnp.dot(w, vn[:, sl], preferred_element_type=F32) + bs[:, h:h + 1]
        o_ref[0, :, sl] = (u[:, sl] * mixed).astype(o_ref.dtype)


def _sgu(proj3, ln_g, ln_b, w_s, b_st):
    b, t, _ = proj3.shape
    gs, tc, _ = w_s.shape
    vec = pl.BlockSpec((1, SGU_WIDTH), lambda i, j: (0, 0))
    return pl.pallas_call(
        _sgu_kernel,
        grid=(b, t // tc),
        in_specs=[pl.BlockSpec((1, tc, SGU_WIDTH), lambda i, j: (i, j, 0)),
                  pl.BlockSpec((1, tc, SGU_WIDTH), lambda i, j: (i, j, 1)),
                  vec, vec,
                  pl.BlockSpec((gs, tc, tc), lambda i, j: (0, 0, 0)),
                  pl.BlockSpec((tc, gs), lambda i, j: (0, 0))],
        out_specs=pl.BlockSpec((1, tc, SGU_WIDTH), lambda i, j: (i, j, 0)),
        out_shape=jax.ShapeDtypeStruct((b, t, SGU_WIDTH), MXU_DTYPE),
        compiler_params=_cparams(("arbitrary", "arbitrary")),
        name="sgu",
    )(proj3, proj3, ln_g, ln_b, w_s, b_st)


def _merge_kernel(ya_ref, yb_ref, wa_ref, wb_ref, ga_ref, gb_ref, o_ref):
    pa = jnp.dot(ya_ref[...], wa_ref[...], preferred_element_type=F32)
    pb = jnp.dot(yb_ref[...], wb_ref[...], preferred_element_type=F32)
    merged = (jax.nn.sigmoid(ga_ref[...].astype(F32)) * pa + jax.nn.sigmoid(gb_ref[...].astype(F32)) * pb)
    o_ref[...] = merged.astype(o_ref.dtype)


def _merge(ya, yb, wa, wb, proj, ga_col, gb_col, *, tm=1024, tn=512):
    m, ka = ya.shape
    kb = yb.shape[1]
    n = wa.shape[1]
    return pl.pallas_call(
        _merge_kernel,
        grid=(m // tm, n // tn),
        in_specs=[pl.BlockSpec((tm, ka), lambda i, j: (i, 0)),
                  pl.BlockSpec((tm, kb), lambda i, j: (i, 0)),
                  pl.BlockSpec((ka, tn), lambda i, j: (0, j)),
                  pl.BlockSpec((kb, tn), lambda i, j: (0, j)),
                  pl.BlockSpec((tm, tn), lambda i, j: (i, ga_col + j)),
                  pl.BlockSpec((tm, tn), lambda i, j: (i, gb_col + j))],
        out_specs=pl.BlockSpec((tm, tn), lambda i, j: (i, j)),
        out_shape=jax.ShapeDtypeStruct((m, n), MXU_DTYPE),
        compiler_params=_cparams(("arbitrary", "arbitrary")),
        name="merge",
    )(ya, yb, wa, wb, proj, proj)


def _ffn_up_kernel(h_ref, halo_ref, wg_ref, wu_ref, cwg_ref, cwu_ref, cbg_ref, cbu_ref, o_ref, *, seq, sub):
    i = pl.program_id(0)
    tm = h_ref.shape[0]
    wg = wg_ref[...]
    wu = wu_ref[...]
    row = jax.lax.broadcasted_iota(jnp.int32, (sub, wg.shape[1]), 0)

    def conv(a, before, cw_ref, cb_ref):
        a1 = jnp.where(row == 0, before[7:8, :], pltpu.roll(a, 1, 0))
        a2 = jnp.where(row == 0, before[6:7, :], jnp.where(row == 1, before[7:8, :], pltpu.roll(a, 2, 0)))
        return cb_ref[0:1, :] + a2 * cw_ref[0:1, :] + a1 * cw_ref[1:2, :] + a * cw_ref[2:3, :]

    keep = jnp.where((i * tm) % seq == 0, 0.0, 1.0)
    before_g = jnp.dot(halo_ref[...], wg, preferred_element_type=F32) * keep
    before_u = jnp.dot(halo_ref[...], wu, preferred_element_type=F32) * keep
    for s in range(tm // sub):
        hs = h_ref[s * sub:(s + 1) * sub, :]
        ag = jnp.dot(hs, wg, preferred_element_type=F32)
        au = jnp.dot(hs, wu, preferred_element_type=F32)
        gate = conv(ag, before_g, cwg_ref, cbg_ref)
        up = conv(au, before_u, cwu_ref, cbu_ref)
        o_ref[s * sub:(s + 1) * sub, :] = (gate * jax.nn.sigmoid(gate) * up).astype(o_ref.dtype)
        before_g = ag[sub - 8:, :]
        before_u = au[sub - 8:, :]


def _ffn_up(h, w, cw8, cb8, *, seq, tm=2048, sub=1024):
    m, d = h.shape
    f = w.shape[1] // 2
    tn = FFN_HALF_TILE
    nj = f // tn
    halo_blocks = tm // 8
    assert seq % tm == 0 and tm % sub == 0
    gate_cols = lambda rows: pl.BlockSpec((rows, tn), lambda i, j: (0, j))
    up_cols = lambda rows: pl.BlockSpec((rows, tn), lambda i, j: (0, nj + j))
    return pl.pallas_call(
        functools.partial(_ffn_up_kernel, seq=seq, sub=sub),
        grid=(m // tm, nj),
        in_specs=[pl.BlockSpec((tm, d), lambda i, j: (i, 0)),
                  pl.BlockSpec((8, d), lambda i, j: (jnp.maximum(i * halo_blocks - 1, 0), 0)),
                  gate_cols(d), up_cols(d), gate_cols(8), up_cols(8), gate_cols(8), up_cols(8)],
        out_specs=pl.BlockSpec((tm, tn), lambda i, j: (i, j)),
        out_shape=jax.ShapeDtypeStruct((m, f), MXU_DTYPE),
        compiler_params=_cparams(("arbitrary", "arbitrary")),
        name="ffn_up",
    )(h, h, w, w, cw8, cw8, cb8, cb8)


def _pad_rows(a, rows):
    return jnp.pad(a, ((0, rows - a.shape[0]), (0, 0)))


def _mixer_branches(h, w_in, cmp_pos, cmp_w1, cmp_b1, cmp_w2, cmp_b2, rel_table,
                    sgu_ln_g, sgu_ln_b, sgu_w, sgu_b):
    b, t, d = h.shape
    g, r, dk = NSA_KV_GROUPS, HEADS_PER_GROUP, HEAD_DIM
    m = b * t
    nq = t // Q_BLOCK
    n_cmp = (t - CMP_BLOCK) // CMP_STRIDE + 1
    n_slc = t // SLC_BLOCK
    rows16 = t // CMP_STRIDE
    ncp = rows16 + LANES

    o_gn = NSA_WIDTH + 6 * KV_WIDTH
    o_z = o_gn + NSA_HEADS * 3
    w_in_t = w_in.T
    w_att = w_in_t[:o_gn].astype(MXU_DTYPE)
    w_gn = jnp.pad(w_in_t[o_gn:o_z], ((0, LANES - NSA_HEADS * 3), (0, 0))).astype(MXU_DTYPE)
    w_rest = w_in_t[o_z:].astype(MXU_DTYPE)
    c_ga, c_gb = 2 * SGU_WIDTH, 2 * SGU_WIDTH + d
    c_kv = [NSA_WIDTH + i * KV_WIDTH for i in range(6)]
    col_scale = jnp.ones((1, o_gn), F32).at[:, :NSA_WIDTH].set(HEAD_DIM ** -0.5 * LOG2E)

    h2d = h.reshape(m, d)
    proj = _matmul(h2d, w_rest, tm=1024, tn=1024, out_dtype=MXU_DTYPE, w_is_nk=True, name="proj_rest")
    patt = _matmul(h2d, w_att, tm=1024, tn=1024, out_dtype=MXU_DTYPE, col_scale=col_scale, w_is_nk=True,
                   name="proj_att")
    gn = _matmul(h2d, w_gn, tm=1024, tn=LANES, out_dtype=F32, w_is_nk=True, name="proj_gates")
    proj3 = proj.reshape(b, t, proj.shape[1])
    patt3 = patt.reshape(b, t, o_gn)

    kv_c = patt3[:, :, c_kv[0]:c_kv[2]].reshape(b, rows16, CMP_STRIDE, 2, g, dk)
    a_cmp = kv_c.transpose(0, 3, 4, 1, 2, 5).reshape(b, 2, g, rows16, CMP_STRIDE * dk)
    pos8 = jnp.broadcast_to(cmp_pos.reshape(2, 1, CMP_BLOCK * dk), (2, 8, CMP_BLOCK * dk))
    cmp_out = _compress(a_cmp, cmp_w1.astype(MXU_DTYPE), pos8, cmp_b1.reshape(2, 1, CMP_HIDDEN),
                        cmp_w2.astype(MXU_DTYPE), cmp_b2.reshape(2, 1, dk), n_cmp)
    cmp_pad = jnp.pad(cmp_out, ((0, 0), (0, 0), (0, 0), (CMP_FRONT, ncp - CMP_FRONT - rows16), (0, 0)))

    gt = gn[:, :NSA_HEADS * 3].reshape(b, nq, Q_BLOCK, g, r, 3).transpose(0, 3, 1, 5, 4, 2).reshape(b, g, nq, 3, QL)
    gt = jnp.pad(gt, ((0, 0), (0, 0), (0, 0), (0, 5), (0, 0)))
    bwin_full, bwin_shifted, bc, c31 = _bias_tiles(rel_table)
    bwin = _bias_variants(bwin_full, WINDOW // Q_BLOCK + 1)
    bnear = _bias_variants(bwin_shifted, 2)
    selt = jnp.asarray(_selection_map_t(n_cmp, n_slc, ncp), MXU_DTYPE)
    y_a = _nsa(patt3, cmp_pad, selt, c_kv[2] // dk, gt, bnear, bwin, bc, c31)

    y_b = _sgu(proj3, sgu_ln_g.reshape(1, SGU_WIDTH), sgu_ln_b.reshape(1, SGU_WIDTH), sgu_w, sgu_b.T)
    return proj, y_a, y_b, c_ga, c_gb


def _token_mixer(h, w_in, cmp_pos, cmp_w1, cmp_b1, cmp_w2, cmp_b2, rel_table,
                 sgu_ln_g, sgu_ln_b, sgu_w, sgu_b, w_proj_nsa, w_proj_sgu):
    proj, y_a, y_b, c_ga, c_gb = _mixer_branches(h, w_in, cmp_pos, cmp_w1, cmp_b1, cmp_w2, cmp_b2, rel_table,
                                                 sgu_ln_g, sgu_ln_b, sgu_w, sgu_b)
    m = proj.shape[0]
    tn = 1024
    return _merge(y_a.reshape(m, NSA_WIDTH), y_b.reshape(m, SGU_WIDTH),
                  w_proj_nsa.astype(MXU_DTYPE), w_proj_sgu.astype(MXU_DTYPE),
                  proj, c_ga // tn, c_gb // tn, tm=1024, tn=tn)


def kernel(x, c, w_mod, b_mod, g_norms, w_in, cmp_pos, cmp_w1, cmp_b1, cmp_w2, cmp_b2, rel_table, sgu_ln_g, sgu_ln_b, sgu_w, sgu_b, w_proj_nsa, w_proj_sgu, w_out, w_ffn_up, ffn_conv_w, ffn_conv_b, w_ffn_down):
    b, t, d = x.shape
    m = b * t
    depth = w_mod.shape[0]
    c8 = jnp.pad(c, ((0, 8 - b), (0, 0)))
    for l in range(depth):
        mod = _mod(c8, w_mod[l], b_mod[l].reshape(1, -1))[:b]
        sh1, sc1, gt1, sh2, sc2, gt2 = [v.reshape(b, 1, d) for v in jnp.split(mod, 6, axis=-1)]
        gn = g_norms[l]

        h = _norm_mod(x, gn[0:1], sh1, sc1)
        merged = _token_mixer(h, w_in[l], cmp_pos[l], cmp_w1[l], cmp_b1[l], cmp_w2[l], cmp_b2[l], rel_table,
                              sgu_ln_g[l], sgu_ln_b[l], sgu_w[l], sgu_b[l], w_proj_nsa[l], w_proj_sgu[l])
        y = _matmul(merged, w_out[l].astype(MXU_DTYPE), tm=1024, tn=1024, out_dtype=MXU_DTYPE, name="out_proj")
        x1, h2 = _resid_norm_mod(x, y.reshape(b, t, d), gn[1:2], gt1, gn[2:3], sh2, sc2)

        cw8 = _pad_rows(ffn_conv_w[l], 8)
        cb8 = _pad_rows(ffn_conv_b[l].reshape(1, -1), 8)
        act = _ffn_up(h2.reshape(m, d), w_ffn_up[l].astype(MXU_DTYPE), cw8, cb8, seq=t)
        y2 = _matmul(act, w_ffn_down[l].astype(MXU_DTYPE), tm=512, tn=512, out_dtype=MXU_DTYPE,
                     name="ffn_down")
        x = _resid_norm(x1, y2.reshape(b, t, d), gn[3:4], gt2)
    return x
```
